```python
import math
import jax, jax.numpy as jnp
from jax import lax
import numpy as np

D_MODEL = 1024
BATCH = 8
SEQ = 2048
DEPTH = 1
DEC_BATCH = 128
DEC_SEQ = 1
PAST_LEN = 16384
PAGE_SIZE = 128

P_DIM = 256
M_HEADS = 4
M_WIDTH = D_MODEL
M_HEAD_DIM = M_WIDTH // M_HEADS
C_WIDTH = D_MODEL
CONV_W = 3
D_FF = (11 * D_MODEL) // 4
CHUNK = 64
LN_EPS = 1e-5
ALPHA = (2 * DEPTH) ** 0.25
BETA = (8 * DEPTH) ** -0.25
F_BIAS_LO = 3.0
F_BIAS_HI = 6.0

Q0 = 0
K0 = Q0 + M_WIDTH
V0 = K0 + M_WIDTH
O0 = V0 + M_WIDTH
I0 = O0 + M_WIDTH
F0 = I0 + M_HEADS
CB0 = F0 + M_HEADS
CC0 = CB0 + C_WIDTH
CH0 = CC0 + C_WIDTH
GA0 = CH0 + C_WIDTH
GB0 = GA0 + D_MODEL
N_IN = GB0 + D_MODEL

kernel_name = "hybrid_mlstm_shortconv_deepnorm_step"


def layer_norm(x, g, b):
    x32 = x.astype(jnp.float32)
    mu = x32.mean(-1, keepdims=True)
    var = jnp.square(x32 - mu).mean(-1, keepdims=True)
    out = (x32 - mu) * lax.rsqrt(var + LN_EPS) * g.astype(jnp.float32) + b.astype(jnp.float32)
    return out.astype(x.dtype)


def head_norm(h, g):
    mu = h.mean(-1, keepdims=True)
    var = jnp.square(h - mu).mean(-1, keepdims=True)
    return (h - mu) * lax.rsqrt(var + LN_EPS) * g.astype(jnp.float32)


def swiglu(x, wi, wo):
    g, u = jnp.split(x @ wi, 2, axis=-1)
    return (jax.nn.silu(g) * u) @ wo


def mlstm_chunkwise(q, k, v, log_i, log_f, C0, n0, m0):
    B, L, H, Dh = q.shape
    lc = math.gcd(L, CHUNK)
    nc = L // lc

    def to_chunks(a):
        a = a.reshape((B, nc, lc, H) + a.shape[3:])
        return jnp.moveaxis(a, (1, 3), (0, 2))

    tril = jnp.tril(jnp.ones((lc, lc), dtype=bool))

    def body(carry, inp):
        C, n, m = carry
        qc, kc, vc, li, lf = inp
        b = jnp.cumsum(lf, axis=-1)
        dmat = b[..., :, None] - b[..., None, :] + li[..., None, :]
        dmat = jnp.where(tril, dmat, float('-inf'))
        inter = b + m[..., None]
        m_row = jnp.maximum(inter, dmat.max(-1))
        w_inter = jnp.exp(inter - m_row)
        s = jnp.einsum('bhtd,bhsd->bhts', qc, kc) * jnp.exp(dmat - m_row[..., None])
        num = (w_inter[..., None] * jnp.einsum('bhtd,bhde->bhte', qc, C)
               + jnp.einsum('bhts,bhse->bhte', s, vc))
        den = w_inter * jnp.einsum('bhtd,bhd->bht', qc, n) + s.sum(-1)
        h = num / jnp.maximum(jnp.abs(den), jnp.exp(-m_row))[..., None]
        b_end = b[..., -1]
        dec = b_end[..., None] - b + li
        m_new = jnp.maximum(b_end + m, dec.max(-1))
        w_c = jnp.exp(b_end + m - m_new)
        kw = kc * jnp.exp(dec - m_new[..., None])[..., None]
        C_new = w_c[..., None, None] * C + jnp.einsum('bhsd,bhse->bhde', kw, vc)
        n_new = w_c[..., None] * n + kw.sum(-2)
        return (C_new, n_new, m_new), h

    init = (C0.astype(jnp.float32), n0.astype(jnp.float32), m0.astype(jnp.float32))
    (C, n, m), h = lax.scan(body, init, (to_chunks(q), to_chunks(k), to_chunks(v),
                                         to_chunks(log_i), to_chunks(log_f)))
    h = jnp.moveaxis(h, (0, 2), (1, 3)).reshape(B, L, H, Dh)
    return h, C, n, m


def short_conv(pre, buf, w, b):
    L = pre.shape[1]
    full = jnp.concatenate([buf.astype(pre.dtype), pre], axis=1)
    y = b
    for j in range(CONV_W):
        y = y + full[:, j:j + L] * w[j]
    return y, full[:, L:]


def decoder_layer(x, p, C0, n0, m0, conv_buf, w_in, b_in, m_norm_g, w_a, w_b, conv_w, conv_b,
                  w_mix, ffn1_wi, ffn1_wo, ffn2_wi, ffn2_wo, w_pg, w_pp, ln_g, ln_b):
    Bn, L, _ = x.shape
    f32 = jnp.float32
    x = layer_norm(ALPHA * x + 0.5 * swiglu(x, ffn1_wi, ffn1_wo), ln_g[0], ln_b[0])
    z = x @ w_in + b_in
    hs = (Bn, L, M_HEADS, M_HEAD_DIM)
    q = z[..., Q0:K0].reshape(hs).astype(f32)
    k = z[..., K0:V0].reshape(hs).astype(f32) * (M_HEAD_DIM ** -0.5)
    v = z[..., V0:O0].reshape(hs).astype(f32)
    log_i = z[..., I0:F0].astype(f32)
    log_f = jax.nn.log_sigmoid(z[..., F0:CB0].astype(f32))
    h, C, n, m = mlstm_chunkwise(q, k, v, log_i, log_f, C0, n0, m0)
    hn = head_norm(h, m_norm_g).reshape(Bn, L, M_WIDTH).astype(x.dtype)
    y_a = (hn * jax.nn.sigmoid(z[..., O0:I0])) @ w_a
    pre = z[..., CC0:CH0] * z[..., CH0:GA0]
    conv, new_buf = short_conv(pre, conv_buf, conv_w, conv_b)
    y_b = (z[..., CB0:CC0] * conv) @ w_b
    mix = (jax.nn.sigmoid(z[..., GA0:GB0]) * y_a + jax.nn.sigmoid(z[..., GB0:N_IN]) * y_b) @ w_mix
    x = layer_norm(ALPHA * x + mix, ln_g[1], ln_b[1])
    x = layer_norm(ALPHA * x + 0.5 * swiglu(x, ffn2_wi, ffn2_wo), ln_g[2], ln_b[2])
    x = layer_norm(ALPHA * x + jax.nn.sigmoid(x @ w_pg) * (p @ w_pp), ln_g[3], ln_b[3])
    return x, C, n, m, new_buf


def setup_inputs(seed: int = 0) -> dict:
    key = jax.random.key(seed)
    ks = jax.random.split(key, 32)

    def nrm(k, shape, scale):
        return jax.random.normal(k, shape, jnp.float32) * scale

    b_in = nrm(ks[10], (DEPTH, N_IN), 0.02)
    f_bias = jnp.linspace(F_BIAS_LO, F_BIAS_HI, M_HEADS, dtype=jnp.float32)[None, :] + nrm(ks[11], (DEPTH, M_HEADS), 0.1)
    b_in = b_in.at[:, F0:CB0].set(f_bias)
    return {
        "x_prompt": nrm(ks[0], (BATCH, SEQ, D_MODEL), 1.0),
        "x_sample": nrm(ks[1], (DEC_BATCH, DEC_SEQ, D_MODEL), 1.0),
        "p_prompt": nrm(ks[2], (DEPTH, BATCH, SEQ, P_DIM), 1.0),
        "p_sample": nrm(ks[3], (DEPTH, DEC_BATCH, DEC_SEQ, P_DIM), 1.0),
        "state_C": nrm(ks[4], (DEPTH, DEC_BATCH, M_HEADS, M_HEAD_DIM, M_HEAD_DIM), M_HEAD_DIM ** -0.5),
        "state_n": nrm(ks[5], (DEPTH, DEC_BATCH, M_HEADS, M_HEAD_DIM), M_HEAD_DIM ** -0.5),
        "state_m": nrm(ks[6], (DEPTH, DEC_BATCH, M_HEADS), 1.0),
        "state_conv": nrm(ks[7], (DEPTH, DEC_BATCH, CONV_W - 1, C_WIDTH), 1.0),
        "w_in": nrm(ks[8], (DEPTH, D_MODEL, N_IN), D_MODEL ** -0.5),
        "b_in": b_in,
        "m_norm_g": 1.0 + nrm(ks[12], (DEPTH, M_HEADS, M_HEAD_DIM), 0.02),
        "w_a": nrm(ks[13], (DEPTH, M_WIDTH, D_MODEL), M_WIDTH ** -0.5),
        "w_b": nrm(ks[14], (DEPTH, C_WIDTH, D_MODEL), C_WIDTH ** -0.5),
        "conv_w": nrm(ks[15], (DEPTH, CONV_W, C_WIDTH), CONV_W ** -0.5),
        "conv_b": nrm(ks[16], (DEPTH, C_WIDTH), 0.02),
        "w_mix": nrm(ks[17], (DEPTH, D_MODEL, D_MODEL), BETA * D_MODEL ** -0.5),
        "ffn1_wi": nrm(ks[18], (DEPTH, D_MODEL, 2 * D_FF), D_MODEL ** -0.5),
        "ffn1_wo": nrm(ks[19], (DEPTH, D_FF, D_MODEL), BETA * D_FF ** -0.5),
        "ffn2_wi": nrm(ks[20], (DEPTH, D_MODEL, 2 * D_FF), D_MODEL ** -0.5),
        "ffn2_wo": nrm(ks[21], (DEPTH, D_FF, D_MODEL), BETA * D_FF ** -0.5),
        "w_pg": nrm(ks[22], (DEPTH, D_MODEL, D_MODEL), D_MODEL ** -0.5),
        "w_pp": nrm(ks[23], (DEPTH, P_DIM, D_MODEL), BETA * P_DIM ** -0.5),
        "ln_g": 1.0 + nrm(ks[24], (DEPTH, 4, D_MODEL), 0.02),
        "ln_b": nrm(ks[25], (DEPTH, 4, D_MODEL), 0.02),
    }


def reference(x_prompt, x_sample, p_prompt, p_sample, state_C, state_n, state_m, state_conv,
              w_in, b_in, m_norm_g, w_a, w_b, conv_w, conv_b, w_mix,
              ffn1_wi, ffn1_wo, ffn2_wi, ffn2_wo, w_pg, w_pp, ln_g, ln_b):
    xp, xs = x_prompt, x_sample
    nb = x_prompt.shape[0]
    Cp_l, np_l, mp_l, bp_l = [], [], [], []
    Cs_l, ns_l, ms_l, bs_l = [], [], [], []
    for i in range(DEPTH):
        wl = (w_in[i], b_in[i], m_norm_g[i], w_a[i], w_b[i], conv_w[i], conv_b[i], w_mix[i],
              ffn1_wi[i], ffn1_wo[i], ffn2_wi[i], ffn2_wo[i], w_pg[i], w_pp[i], ln_g[i], ln_b[i])
        C0 = jnp.zeros((nb, M_HEADS, M_HEAD_DIM, M_HEAD_DIM), jnp.float32)
        n0 = jnp.zeros((nb, M_HEADS, M_HEAD_DIM), jnp.float32)
        m0 = jnp.zeros((nb, M_HEADS), jnp.float32)
        buf0 = jnp.zeros((nb, CONV_W - 1, C_WIDTH), xp.dtype)
        xp, Cp, np_, mp, bp = decoder_layer(xp, p_prompt[i], C0, n0, m0, buf0, *wl)
        xs, Cs, ns, ms, bs = decoder_layer(xs, p_sample[i], state_C[i], state_n[i], state_m[i],
                                           state_conv[i], *wl)
        Cp_l.append(Cp); np_l.append(np_); mp_l.append(mp); bp_l.append(bp)
        Cs_l.append(Cs); ns_l.append(ns); ms_l.append(ms); bs_l.append(bs)
    C_prompt = jnp.stack(Cp_l)
    n_prompt = jnp.stack(np_l)
    m_prompt = jnp.stack(mp_l)
    conv_prompt = jnp.stack(bp_l)
    C_sample = jnp.stack(Cs_l)
    n_sample = jnp.stack(ns_l)
    m_sample = jnp.stack(ms_l)
    conv_sample = jnp.stack(bs_l)
    return (xp, xs, C_prompt, n_prompt, m_prompt, conv_prompt, C_sample, n_sample, m_sample, conv_sample)
```

```python
import functools

import jax
import jax.numpy as jnp
from jax import lax
from jax.experimental import pallas as pl
from jax.experimental.pallas import tpu as pltpu

F32 = jnp.float32
BF16 = jnp.bfloat16

LN_EPS = 1e-5
CONV_W = 3
LANE = 128
SUBLANE = 8
V7X_VMEM_LIMIT_BYTES = 56 * 1024 * 1024
MLSTM_CHUNK = 256
FFN_ROWS = 512
FFN_COLS = 256


def _dot(a, b):
    return jnp.dot(a.astype(BF16), b.astype(BF16), preferred_element_type=F32)


def _dot_nt(a, b):
    return lax.dot_general(a.astype(BF16), b.astype(BF16), (((1,), (1,)), ((), ())),
                           preferred_element_type=F32)


def _dot_tn(a, b):
    return lax.dot_general(a.astype(BF16), b.astype(BF16), (((0,), (0,)), ((), ())),
                           preferred_element_type=F32)


def _split3(x):
    hi = x.astype(BF16)
    r1 = x - hi.astype(F32)
    mid = r1.astype(BF16)
    lo = (r1 - mid.astype(F32)).astype(BF16)
    return hi, mid, lo


def _layer_norm(x, g, b):
    mu = jnp.mean(x, axis=-1, keepdims=True)
    xc = x - mu
    var = jnp.mean(xc * xc, axis=-1, keepdims=True)
    return xc * lax.rsqrt(var + LN_EPS) * g + b


def _head_norm(h, g):
    mu = jnp.mean(h, axis=-1, keepdims=True)
    hc = h - mu
    var = jnp.mean(hc * hc, axis=-1, keepdims=True)
    return hc * lax.rsqrt(var + LN_EPS) * g


def _sigmoid(x):
    return 1.0 / (1.0 + jnp.exp(-x))


def _log_sigmoid(x):
    return jnp.minimum(x, 0.0) - jnp.log1p(jnp.exp(-jnp.abs(x)))


def _ffn_kernel(*refs, alpha, n_ff_steps, ln_row, with_embed):
    if with_embed:
        (x_ref, p_ref, wg_ref, wu_ref, wo_ref, wpg_ref, wpp_ref, lng_ref, lnb_ref, o_ref) = refs
    else:
        (x_ref, wg_ref, wu_ref, wo_ref, lng_ref, lnb_ref, o_ref) = refs
    x = x_ref[...]
    xb = x.astype(BF16)
    acc = jnp.zeros(x.shape, F32)
    for c in range(n_ff_steps):
        g = jnp.dot(xb, wg_ref[c], preferred_element_type=F32)
        u = jnp.dot(xb, wu_ref[c], preferred_element_type=F32)
        a = (g * _sigmoid(g)) * u
        acc = acc + jnp.dot(a.astype(BF16), wo_ref[c], preferred_element_type=F32)
    y = _layer_norm(alpha * x + 0.5 * acc, lng_ref[ln_row:ln_row + 1, :], lnb_ref[ln_row:ln_row + 1, :])
    if with_embed:
        gate = _sigmoid(jnp.dot(y.astype(BF16), wpg_ref[...], preferred_element_type=F32))
        emb = jnp.dot(p_ref[...].astype(BF16), wpp_ref[...], preferred_element_type=F32)
        y = _layer_norm(alpha * y + gate * emb, lng_ref[ln_row + 1:ln_row + 2, :],
                        lnb_ref[ln_row + 1:ln_row + 2, :])
    o_ref[...] = y


def _const_spec(shape):
    nd = len(shape)
    return pl.BlockSpec(shape, lambda *_: (0,) * nd, pipeline_mode=pl.Buffered(1))


def _ffn_call(x, p, wg, wu, wo, wpg, wpp, lng, lnb, *, alpha, ln_row, with_embed, name):
    n, d = x.shape
    rows = min(FFN_ROWS, n)
    assert n % rows == 0
    row_spec = lambda w: pl.BlockSpec((rows, w), lambda i: (i, 0))
    if with_embed:
        args = (x, p, wg, wu, wo, wpg, wpp, lng, lnb)
        in_specs = [row_spec(d), row_spec(p.shape[1])] + [_const_spec(a.shape) for a in args[2:]]
    else:
        args = (x, wg, wu, wo, lng, lnb)
        in_specs = [row_spec(d)] + [_const_spec(a.shape) for a in args[1:]]
    kern = functools.partial(_ffn_kernel, alpha=alpha, n_ff_steps=wg.shape[0], ln_row=ln_row,
                             with_embed=with_embed)
    return pl.pallas_call(
        kern,
        grid=(n // rows,),
        in_specs=in_specs,
        out_specs=row_spec(d),
        out_shape=jax.ShapeDtypeStruct((n, d), F32),
        compiler_params=pltpu.CompilerParams(dimension_semantics=("arbitrary",),
                                             vmem_limit_bytes=V7X_VMEM_LIMIT_BYTES),
        name=name,
    )(*args)


def _merge_tail(x, ha, hb, z_ga, z_gb, wa_ref, wb_ref, wmix_ref, lng_ref, lnb_ref, alpha):
    y_a = jnp.dot(ha.astype(BF16), wa_ref[...], preferred_element_type=F32)
    y_b = jnp.dot(hb.astype(BF16), wb_ref[...], preferred_element_type=F32)
    merged = _sigmoid(z_ga) * y_a + _sigmoid(z_gb) * y_b
    mix = jnp.dot(merged.astype(BF16), wmix_ref[...], preferred_element_type=F32)
    return _layer_norm(alpha * x + mix, lng_ref[1:2, :], lnb_ref[1:2, :])


def _prompt_mix_kernel(x_ref, w_ref, b_ref, wift_ref, bift_ref, mng_ref, convw_ref, convb_ref,
                       wa_ref, wb_ref, wmix_ref, lng_ref, lnb_ref,
                       o_ref, c_ref, n_ref, m_ref, conv_ref, pre_ref,
                       *, alpha, heads, hdim, width):
    t = pl.program_id(1)
    lc = x_ref.shape[1]
    mw = heads * hdim
    col = dict(q=0, k=mw, v=2 * mw, o=3 * mw, g=4 * mw, cb=4 * mw + LANE)
    col.update(cc=col["cb"] + width, ch=col["cb"] + 2 * width, ga=col["cb"] + 3 * width)
    col.update(gb=col["ga"] + x_ref.shape[2])
    carry0 = SUBLANE - (CONV_W - 1)

    @pl.when(t == 0)
    def _():
        c_ref[...] = jnp.zeros(c_ref.shape, F32)
        n_ref[...] = jnp.zeros(n_ref.shape, F32)
        m_ref[...] = jnp.zeros(m_ref.shape, F32)
        pre_ref[0:SUBLANE, :] = jnp.zeros((SUBLANE, width), F32)

    x = x_ref[0]
    xb = x.astype(BF16)

    def proj(c0, w):
        return jnp.dot(xb, w_ref[:, c0:c0 + w], preferred_element_type=F32) + b_ref[:, c0:c0 + w]

    gcol = proj(col["g"], LANE)
    grow = _dot_nt(wift_ref[...], xb) + bift_ref[:, 0:1]
    lf_col = _log_sigmoid(gcol)
    lf_row = _log_sigmoid(grow)
    r_id = lax.broadcasted_iota(jnp.int32, (lc, lc), 0)
    c_id = lax.broadcasted_iota(jnp.int32, (lc, lc), 1)
    causal = c_id <= r_id
    lower = causal.astype(BF16)
    upper = (r_id <= c_id).astype(BF16)
    bcol = sum(jnp.dot(lower, part, preferred_element_type=F32) for part in _split3(lf_col))
    brow = sum(jnp.dot(part, upper, preferred_element_type=F32) for part in _split3(lf_row))

    m_all = m_ref[0]
    lane_id = lax.broadcasted_iota(jnp.int32, (1, LANE), 1)
    m_next = m_all
    ha_parts = []
    for h in range(heads):
        hs = slice(h * hdim, (h + 1) * hdim)
        q = proj(col["q"] + h * hdim, hdim)
        k = proj(col["k"] + h * hdim, hdim) * (hdim ** -0.5)
        v = proj(col["v"] + h * hdim, hdim)
        b_c = bcol[:, heads + h:heads + h + 1]
        li_c = gcol[:, h:h + 1]
        b_r = brow[heads + h:heads + h + 1, :]
        li_r = grow[h:h + 1, :]
        m_prev = m_all[:, h:h + 1]
        c_h = c_ref[0, h]
        n_h = n_ref[0, :, hs]

        dmat = jnp.where(causal, b_c - b_r + li_r, -jnp.inf)
        inter = b_c + m_prev
        m_row = jnp.maximum(inter, jnp.max(dmat, axis=1, keepdims=True))
        w_inter = jnp.exp(inter - m_row)
        s = _dot_nt(q, k) * jnp.exp(dmat - m_row)
        num = w_inter * _dot(q, c_h) + _dot(s, v)
        den = w_inter * jnp.sum(q * n_h, axis=1, keepdims=True) + jnp.sum(s, axis=1, keepdims=True)
        hh = num * (1.0 / jnp.maximum(jnp.abs(den), jnp.exp(-m_row)))

        b_end = b_c[lc - 1:lc, :]
        dec = b_end - b_c + li_c
        m_new = jnp.maximum(b_end + m_prev, jnp.max(dec, axis=0, keepdims=True))
        w_c = jnp.exp(b_end + m_prev - m_new)
        kw = k * jnp.exp(dec - m_new)
        c_ref[0, h] = w_c * c_h + _dot_tn(kw, v)
        n_ref[0, :, hs] = w_c * n_h + jnp.sum(kw, axis=0, keepdims=True)
        m_next = jnp.where(lane_id == h, m_new, m_next)

        o_gate = _sigmoid(proj(col["o"] + h * hdim, hdim))
        ha_parts.append(_head_norm(hh, mng_ref[:, hs]) * o_gate)
    m_ref[0] = m_next
    ha = jnp.concatenate(ha_parts, axis=1)

    pre = proj(col["cc"], width) * proj(col["ch"], width)
    pre_ref[SUBLANE:SUBLANE + lc, :] = pre
    conv = convb_ref[...] + pre * convw_ref[CONV_W - 1:CONV_W, :]
    for j in range(CONV_W - 1):
        conv = conv + pre_ref[carry0 + j:carry0 + j + lc, :] * convw_ref[j:j + 1, :]
    tail = pre_ref[lc + carry0:lc + SUBLANE, :]
    pre_ref[carry0:SUBLANE, :] = tail
    conv_ref[0] = tail
    hb = proj(col["cb"], width) * conv

    d = x_ref.shape[2]
    o_ref[0] = _merge_tail(x, ha, hb, proj(col["ga"], d), proj(col["gb"], d),
                           wa_ref, wb_ref, wmix_ref, lng_ref, lnb_ref, alpha)


def _prompt_mix_call(x, w_all, b_all, wift, bift, mng, convw, convb, wa, wb, wmix, lng, lnb,
                     *, alpha, heads, hdim):
    nb, seq, d = x.shape
    lc = min(MLSTM_CHUNK, seq)
    assert seq % lc == 0 and lc % LANE == 0
    width = convw.shape[1]
    consts = (w_all, b_all, wift, bift, mng, convw, convb, wa, wb, wmix, lng, lnb)
    kern = functools.partial(_prompt_mix_kernel, alpha=alpha, heads=heads, hdim=hdim, width=width)
    out_shape = (
        jax.ShapeDtypeStruct((nb, seq, d), F32),
        jax.ShapeDtypeStruct((nb, heads, hdim, hdim), F32),
        jax.ShapeDtypeStruct((nb, 1, heads * hdim), F32),
        jax.ShapeDtypeStruct((nb, 1, LANE), F32),
        jax.ShapeDtypeStruct((nb, CONV_W - 1, width), F32),
    )
    out_specs = (
        pl.BlockSpec((1, lc, d), lambda b, t: (b, t, 0)),
        pl.BlockSpec((1, heads, hdim, hdim), lambda b, t: (b, 0, 0, 0)),
        pl.BlockSpec((1, 1, heads * hdim), lambda b, t: (b, 0, 0)),
        pl.BlockSpec((1, 1, LANE), lambda b, t: (b, 0, 0)),
        pl.BlockSpec((1, CONV_W - 1, width), lambda b, t: (b, 0, 0)),
    )
    return pl.pallas_call(
        kern,
        grid=(nb, seq // lc),
        in_specs=[pl.BlockSpec((1, lc, d), lambda b, t: (b, t, 0))] + [_const_spec(a.shape) for a in consts],
        out_specs=out_specs,
        out_shape=out_shape,
        scratch_shapes=[pltpu.VMEM((lc + SUBLANE, width), F32)],
        compiler_params=pltpu.CompilerParams(dimension_semantics=("arbitrary", "arbitrary"),
                                             vmem_limit_bytes=V7X_VMEM_LIMIT_BYTES),
        name="prompt_mix",
    )(x, *consts)


def _proj_kernel(x_ref, w_ref, b_ref, o_ref):
    o_ref[...] = jnp.dot(x_ref[...].astype(BF16), w_ref[...], preferred_element_type=F32) + b_ref[...]


def _proj_call(x, w_all, b_all):
    n = x.shape[0]
    return pl.pallas_call(
        _proj_kernel,
        out_shape=jax.ShapeDtypeStruct((n, w_all.shape[1]), F32),
        compiler_params=pltpu.CompilerParams(vmem_limit_bytes=V7X_VMEM_LIMIT_BYTES),
        name="sample_proj",
    )(x, w_all, b_all)


def _sample_step_kernel(z_ref, c_ref, n_ref, m_ref, buf_ref, mng_ref, convw_ref, convb_ref,
                        ha_ref, hb_ref, co_ref, no_ref, mo_ref, bo_ref, num_ref,
                        *, heads, hdim, width):
    bb = z_ref.shape[0]
    mw = heads * hdim
    cb0 = 4 * mw + LANE
    zg = z_ref[:, 4 * mw:4 * mw + LANE]
    lane_id = lax.broadcasted_iota(jnp.int32, (bb, LANE), 1)
    m_out = jnp.zeros((bb, LANE), F32)
    reps = LANE // bb
    for h in range(heads):
        hs = slice(h * hdim, (h + 1) * hdim)
        q = z_ref[:, hs]
        k = z_ref[:, mw + h * hdim:mw + (h + 1) * hdim] * (hdim ** -0.5)
        v = z_ref[:, 2 * mw + h * hdim:2 * mw + (h + 1) * hdim]
        li = zg[:, h:h + 1]
        lf = _log_sigmoid(zg[:, heads + h:heads + h + 1])
        m_prev = m_ref[:, h:h + 1]
        m_new = jnp.maximum(lf + m_prev, li)
        w_c = jnp.exp(lf + m_prev - m_new)
        kw = k * jnp.exp(li - m_new)
        n_new = w_c * n_ref[:, hs] + kw
        no_ref[:, hs] = n_new
        m_out = jnp.where(lane_id == h, m_new, m_out)
        q_t = jnp.concatenate([q] * reps, axis=0).T
        kw_t = jnp.concatenate([kw] * reps, axis=0).T
        for j in range(bb):
            c_new = w_c[j:j + 1, :] * c_ref[j, h] + kw_t[:, j:j + 1] * v[j:j + 1, :]
            co_ref[j, h] = c_new
            num_ref[j:j + 1, hs] = jnp.sum(q_t[:, j:j + 1] * c_new, axis=0, keepdims=True)
        den = jnp.sum(q * n_new, axis=1, keepdims=True)
        hh = num_ref[:, hs] * (1.0 / jnp.maximum(jnp.abs(den), jnp.exp(-m_new)))
        o_gate = _sigmoid(z_ref[:, 3 * mw + h * hdim:3 * mw + (h + 1) * hdim])
        ha_ref[:, hs] = _head_norm(hh, mng_ref[:, hs]) * o_gate
    mo_ref[...] = m_out
    pre = z_ref[:, cb0 + width:cb0 + 2 * width] * z_ref[:, cb0 + 2 * width:cb0 + 3 * width]
    conv = convb_ref[...] + pre * convw_ref[CONV_W - 1:CONV_W, :]
    for j in range(CONV_W - 1):
        conv = conv + buf_ref[:, j * width:(j + 1) * width] * convw_ref[j:j + 1, :]
    for j in range(CONV_W - 2):
        bo_ref[:, j * width:(j + 1) * width] = buf_ref[:, (j + 1) * width:(j + 2) * width]
    bo_ref[:, (CONV_W - 2) * width:(CONV_W - 1) * width] = pre
    hb_ref[...] = z_ref[:, cb0:cb0 + width] * conv


def _sample_step_call(z, c0, n0, m0, buf0, mng, convw, convb, *, heads, hdim):
    n = z.shape[0]
    bb = SUBLANE
    assert n % bb == 0
    mw = heads * hdim
    width = convw.shape[1]
    kern = functools.partial(_sample_step_kernel, heads=heads, hdim=hdim, width=width)
    row = lambda w: pl.BlockSpec((bb, w), lambda i: (i, 0))
    c_spec = pl.BlockSpec((bb, heads, hdim, hdim), lambda i: (i, 0, 0, 0))
    out_shape = (
        jax.ShapeDtypeStruct((n, mw), F32), jax.ShapeDtypeStruct((n, width), F32),
        jax.ShapeDtypeStruct(c0.shape, F32), jax.ShapeDtypeStruct((n, mw), F32),
        jax.ShapeDtypeStruct((n, LANE), F32), jax.ShapeDtypeStruct(buf0.shape, F32),
    )
    return pl.pallas_call(
        kern,
        grid=(n // bb,),
        in_specs=[row(z.shape[1]), c_spec, row(mw), row(heads), row(buf0.shape[1]),
                  _const_spec(mng.shape), _const_spec(convw.shape), _const_spec(convb.shape)],
        out_specs=(row(mw), row(width), c_spec, row(mw), row(LANE), row(buf0.shape[1])),
        out_shape=out_shape,
        scratch_shapes=[pltpu.VMEM((bb, mw), F32)],
        compiler_params=pltpu.CompilerParams(dimension_semantics=("arbitrary",),
                                             vmem_limit_bytes=V7X_VMEM_LIMIT_BYTES),
        name="sample_step",
    )(z, c0, n0, m0, buf0, mng, convw, convb)


def _sample_merge_kernel(x_ref, ha_ref, hb_ref, z_ref, wa_ref, wb_ref, wmix_ref, lng_ref, lnb_ref, o_ref,
                         *, alpha, ga0):
    d = x_ref.shape[1]
    o_ref[...] = _merge_tail(x_ref[...], ha_ref[...], hb_ref[...], z_ref[:, ga0:ga0 + d],
                             z_ref[:, ga0 + d:ga0 + 2 * d], wa_ref, wb_ref, wmix_ref, lng_ref, lnb_ref, alpha)


def _sample_merge_call(x, ha, hb, z, wa, wb, wmix, lng, lnb, *, alpha, ga0):
    kern = functools.partial(_sample_merge_kernel, alpha=alpha, ga0=ga0)
    return pl.pallas_call(
        kern,
        out_shape=jax.ShapeDtypeStruct(x.shape, F32),
        compiler_params=pltpu.CompilerParams(vmem_limit_bytes=V7X_VMEM_LIMIT_BYTES),
        name="sample_merge",
    )(x, ha, hb, z, wa, wb, wmix, lng, lnb)


def _ffn_weights(wi, wo):
    d, two_ff = wi.shape
    d_ff = two_ff // 2
    assert d_ff % FFN_COLS == 0
    steps = d_ff // FFN_COLS
    wg = wi[:, :d_ff].reshape(d, steps, FFN_COLS).transpose(1, 0, 2).astype(BF16)
    wu = wi[:, d_ff:].reshape(d, steps, FFN_COLS).transpose(1, 0, 2).astype(BF16)
    return wg, wu, wo.reshape(steps, FFN_COLS, d).astype(BF16)


def kernel(x_prompt, x_sample, p_prompt, p_sample, state_C, state_n, state_m, state_conv, w_in, b_in, m_norm_g, w_a, w_b, conv_w, conv_b, w_mix, ffn1_wi, ffn1_wo, ffn2_wi, ffn2_wo, w_pg, w_pp, ln_g, ln_b):
    depth = w_in.shape[0]
    nb, seq, d = x_prompt.shape
    ns = x_sample.shape[0]
    assert x_sample.shape[1] == 1
    heads, hdim = m_norm_g.shape[1], m_norm_g.shape[2]
    mw = heads * hdim
    width = conv_w.shape[2]
    assert 2 * heads <= SUBLANE
    alpha = (2 * depth) ** 0.25
    gate0 = 4 * mw

    xp = x_prompt
    xs = x_sample.reshape(ns, d)
    outs = [[] for _ in range(8)]
    for i in range(depth):
        wi_l, bi_l = w_in[i], b_in[i]
        pad = LANE - 2 * heads
        w_all = jnp.concatenate([wi_l[:, :gate0], jnp.pad(wi_l[:, gate0:gate0 + 2 * heads], ((0, 0), (0, pad))),
                                 wi_l[:, gate0 + 2 * heads:]], axis=1).astype(BF16)
        b_all = jnp.concatenate([bi_l[:gate0], jnp.pad(bi_l[gate0:gate0 + 2 * heads], (0, pad)),
                                 bi_l[gate0 + 2 * heads:]])[None, :]
        wift = jnp.pad(wi_l[:, gate0:gate0 + 2 * heads].T, ((0, SUBLANE - 2 * heads), (0, 0))).astype(BF16)
        bift = jnp.broadcast_to(jnp.pad(bi_l[gate0:gate0 + 2 * heads], (0, SUBLANE - 2 * heads))[:, None],
                                (SUBLANE, LANE))
        mng = m_norm_g[i].reshape(1, mw)
        convw, convb = conv_w[i], conv_b[i][None, :]
        wa, wb, wmix = w_a[i].astype(BF16), w_b[i].astype(BF16), w_mix[i].astype(BF16)
        wpg, wpp = w_pg[i].astype(BF16), w_pp[i].astype(BF16)
        f1 = _ffn_weights(ffn1_wi[i], ffn1_wo[i])
        f2 = _ffn_weights(ffn2_wi[i], ffn2_wo[i])
        lng, lnb = ln_g[i], ln_b[i]
        ga0 = gate0 + LANE + 3 * width

        x1 = _ffn_call(xp.reshape(nb * seq, d), None, *f1, None, None, lng, lnb, alpha=alpha, ln_row=0,
                       with_embed=False, name="prompt_ffn1")
        x2, c_p, n_p, m_p, buf_p = _prompt_mix_call(x1.reshape(nb, seq, d), w_all, b_all, wift, bift, mng, convw,
                                                    convb, wa, wb, wmix, lng, lnb, alpha=alpha, heads=heads,
                                                    hdim=hdim)
        xp = _ffn_call(x2.reshape(nb * seq, d), p_prompt[i].reshape(nb * seq, -1), *f2, wpg, wpp, lng, lnb,
                       alpha=alpha, ln_row=2, with_embed=True, name="prompt_ffn2").reshape(nb, seq, d)

        s1 = _ffn_call(xs, None, *f1, None, None, lng, lnb, alpha=alpha, ln_row=0, with_embed=False,
                       name="sample_ffn1")
        z = _proj_call(s1, w_all, b_all)
        ha, hb, c_s, n_s, m_s, buf_s = _sample_step_call(
            z, state_C[i], state_n[i].reshape(ns, mw), state_m[i], state_conv[i].reshape(ns, -1),
            mng, convw, convb, heads=heads, hdim=hdim)
        s2 = _sample_merge_call(s1, ha, hb, z, wa, wb, wmix, lng, lnb, alpha=alpha, ga0=ga0)
        xs = _ffn_call(s2, p_sample[i].reshape(ns, -1), *f2, wpg, wpp, lng, lnb, alpha=alpha, ln_row=2,
                       with_embed=True, name="sample_ffn2")

        for lst, val in zip(outs, (c_p, n_p.reshape(nb, heads, hdim), m_p[:, 0, :heads], buf_p,
                                   c_s, n_s.reshape(ns, heads, hdim), m_s[:, :heads],
                                   buf_s.reshape(ns, CONV_W - 1, width))):
            lst.append(val)
    return (xp, xs.reshape(ns, 1, d)) + tuple(jnp.stack(o) for o in outs)
```

```python
import functools

import jax
import jax.numpy as jnp
from jax import lax
from jax.experimental import pallas as pl
from jax.experimental.pallas import tpu as pltpu

F32 = jnp.float32
BF16 = jnp.bfloat16

LN_EPS = 1e-5
CONV_W = 3
LANE = 128
SUBLANE = 8
V7X_VMEM_LIMIT_BYTES = 60 * 1024 * 1024
MLSTM_CHUNK = 256
MIX_ROWS = 512
FFN_ROWS = 1024
FFN_COLS = 256


def _dot(a, b):
    return jnp.dot(a.astype(BF16), b.astype(BF16), preferred_element_type=F32)


def _dot_nt(a, b):
    return lax.dot_general(a.astype(BF16), b.astype(BF16), (((1,), (1,)), ((), ())),
                           preferred_element_type=F32)


def _dot_tn(a, b):
    return lax.dot_general(a.astype(BF16), b.astype(BF16), (((0,), (0,)), ((), ())),
                           preferred_element_type=F32)


def _split3(x):
    hi = x.astype(BF16)
    r1 = x - hi.astype(F32)
    mid = r1.astype(BF16)
    lo = (r1 - mid.astype(F32)).astype(BF16)
    return hi, mid, lo


def _layer_norm(x, g, b):
    mu = jnp.mean(x, axis=-1, keepdims=True)
    xc = x - mu
    var = jnp.mean(xc * xc, axis=-1, keepdims=True)
    return xc * lax.rsqrt(var + LN_EPS) * g + b


def _head_norm(h, g):
    mu = jnp.mean(h, axis=-1, keepdims=True)
    hc = h - mu
    var = jnp.mean(hc * hc, axis=-1, keepdims=True)
    return hc * lax.rsqrt(var + LN_EPS) * g


def _sigmoid(x):
    return 1.0 / (1.0 + jnp.exp(-x))


def _log_sigmoid(x):
    return jnp.minimum(x, 0.0) - jnp.log1p(jnp.exp(-jnp.abs(x)))


def _const_spec(shape):
    nd = len(shape)
    return pl.BlockSpec(shape, lambda *_: (0,) * nd, pipeline_mode=pl.Buffered(1))


def _ffn_kernel(*refs, alpha, ln_row, with_embed):
    if with_embed:
        (x_ref, p_ref, wi_ref, wo_ref, wpg_ref, wpp_ref, lng_ref, lnb_ref, o_ref) = refs
    else:
        (x_ref, wi_ref, wo_ref, lng_ref, lnb_ref, o_ref) = refs
    d_ff = wo_ref.shape[0]
    x = x_ref[...]
    xb = x.astype(BF16)
    acc = jnp.zeros(x.shape, F32)
    for c0 in range(0, d_ff, FFN_COLS):
        g = jnp.dot(xb, wi_ref[:, c0:c0 + FFN_COLS], preferred_element_type=F32)
        u = jnp.dot(xb, wi_ref[:, d_ff + c0:d_ff + c0 + FFN_COLS], preferred_element_type=F32)
        a = (g * _sigmoid(g)) * u
        acc = acc + jnp.dot(a.astype(BF16), wo_ref[c0:c0 + FFN_COLS, :], preferred_element_type=F32)
    y = _layer_norm(alpha * x + 0.5 * acc, lng_ref[ln_row:ln_row + 1, :], lnb_ref[ln_row:ln_row + 1, :])
    if with_embed:
        gate = _sigmoid(jnp.dot(y.astype(BF16), wpg_ref[...], preferred_element_type=F32))
        emb = jnp.dot(p_ref[...].astype(BF16), wpp_ref[...], preferred_element_type=F32)
        y = _layer_norm(alpha * y + gate * emb, lng_ref[ln_row + 1:ln_row + 2, :],
                        lnb_ref[ln_row + 1:ln_row + 2, :])
    o_ref[...] = y


def _ffn_call(x, p, wi, wo, wpg, wpp, lng, lnb, *, alpha, ln_row, with_embed, name):
    n, d = x.shape
    rows = min(FFN_ROWS, n)
    assert n % rows == 0 and wo.shape[0] % FFN_COLS == 0
    row_spec = lambda w: pl.BlockSpec((rows, w), lambda i: (i, 0))
    if with_embed:
        args = (x, p, wi, wo, wpg, wpp, lng, lnb)
        in_specs = [row_spec(d), row_spec(p.shape[1])] + [_const_spec(a.shape) for a in args[2:]]
    else:
        args = (x, wi, wo, lng, lnb)
        in_specs = [row_spec(d)] + [_const_spec(a.shape) for a in args[1:]]
    kern = functools.partial(_ffn_kernel, alpha=alpha, ln_row=ln_row, with_embed=with_embed)
    return pl.pallas_call(
        kern,
        grid=(n // rows,),
        in_specs=in_specs,
        out_specs=row_spec(d),
        out_shape=jax.ShapeDtypeStruct((n, d), F32),
        compiler_params=pltpu.CompilerParams(dimension_semantics=("arbitrary",),
                                             vmem_limit_bytes=V7X_VMEM_LIMIT_BYTES),
        name=name,
    )(*args)


def _merge_tail(x, ha, hb, z_ga, z_gb, wa_ref, wb_ref, wmix_ref, lng_ref, lnb_ref, alpha):
    y_a = jnp.dot(ha.astype(BF16), wa_ref[...], preferred_element_type=F32)
    y_b = jnp.dot(hb.astype(BF16), wb_ref[...], preferred_element_type=F32)
    merged = _sigmoid(z_ga) * y_a + _sigmoid(z_gb) * y_b
    mix = jnp.dot(merged.astype(BF16), wmix_ref[...], preferred_element_type=F32)
    return _layer_norm(alpha * x + mix, lng_ref[1:2, :], lnb_ref[1:2, :])


def _prompt_mix_kernel(x_ref, wm_ref, wg_ref, wc_ref, bm_ref, bg_ref, bc_ref, wgt_ref, bgt_ref,
                       mng_ref, convw_ref, convb_ref, wa_ref, wb_ref, wmix_ref, lng_ref, lnb_ref,
                       o_ref, c_ref, n_ref, m_ref, conv_ref, pre_ref,
                       *, alpha, heads, hdim, width, chunk):
    t = pl.program_id(1)
    rows, d = x_ref.shape[1], x_ref.shape[2]
    mw = heads * hdim
    carry0 = SUBLANE - (CONV_W - 1)

    @pl.when(t == 0)
    def _():
        c_ref[...] = jnp.zeros(c_ref.shape, F32)
        n_ref[...] = jnp.zeros(n_ref.shape, F32)
        m_ref[...] = jnp.zeros(m_ref.shape, F32)
        pre_ref[0:SUBLANE, :] = jnp.zeros((SUBLANE, width), F32)

    x = x_ref[0]
    xb = x.astype(BF16)

    def proj(w_ref, b_ref, c0, w):
        return jnp.dot(xb, w_ref[:, c0:c0 + w], preferred_element_type=F32) + b_ref[:, c0:c0 + w]

    gcol = proj(wg_ref, bg_ref, 0, LANE)
    grow = _dot_nt(wgt_ref[...], xb) + bgt_ref[:, 0:1]
    lf_col = _log_sigmoid(gcol)
    lf_row = _log_sigmoid(grow)
    r_id = lax.broadcasted_iota(jnp.int32, (chunk, chunk), 0)
    c_id = lax.broadcasted_iota(jnp.int32, (chunk, chunk), 1)
    causal = c_id <= r_id
    lower = causal.astype(BF16)
    upper = (r_id <= c_id).astype(BF16)
    lane_id = lax.broadcasted_iota(jnp.int32, (1, LANE), 1)
    starts = list(range(0, rows, chunk))
    b_cols = [sum(jnp.dot(lower, part, preferred_element_type=F32) for part in _split3(lf_col[r0:r0 + chunk]))
              for r0 in starts]
    b_rows = [sum(jnp.dot(part, upper, preferred_element_type=F32) for part in _split3(lf_row[:, r0:r0 + chunk]))
              for r0 in starts]

    ha_parts = []
    for h in range(heads):
        hs = slice(h * hdim, (h + 1) * hdim)
        q_all = proj(wm_ref, bm_ref, h * hdim, hdim)
        k_all = proj(wm_ref, bm_ref, mw + h * hdim, hdim) * (hdim ** -0.5)
        v_all = proj(wm_ref, bm_ref, 2 * mw + h * hdim, hdim)
        hh_parts = []
        for ci, r0 in enumerate(starts):
            rs = slice(r0, r0 + chunk)
            q, k, v = q_all[rs], k_all[rs], v_all[rs]
            b_col = b_cols[ci][:, heads + h:heads + h + 1]
            b_row = b_rows[ci][heads + h:heads + h + 1, :]
            li_c = gcol[rs, h:h + 1]
            li_r = grow[h:h + 1, rs]
            m_all = m_ref[0]
            m_prev = m_all[:, h:h + 1]
            c_h = c_ref[0, h]
            n_h = n_ref[0, :, hs]

            dmat = jnp.where(causal, b_col - b_row + li_r, -jnp.inf)
            inter = b_col + m_prev
            m_row = jnp.maximum(inter, jnp.max(dmat, axis=1, keepdims=True))
            w_inter = jnp.exp(inter - m_row)
            s = _dot_nt(q, k) * jnp.exp(dmat - m_row)
            num = w_inter * _dot(q, c_h) + _dot(s, v)
            den = w_inter * jnp.sum(q * n_h, axis=1, keepdims=True) + jnp.sum(s, axis=1, keepdims=True)
            hh_parts.append(num * (1.0 / jnp.maximum(jnp.abs(den), jnp.exp(-m_row))))

            b_end = b_col[chunk - 1:chunk, :]
            dec = b_end - b_col + li_c
            m_new = jnp.maximum(b_end + m_prev, jnp.max(dec, axis=0, keepdims=True))
            w_c = jnp.exp(b_end + m_prev - m_new)
            kw = k * jnp.exp(dec - m_new)
            c_ref[0, h] = w_c * c_h + _dot_tn(kw, v)
            n_ref[0, :, hs] = w_c * n_h + jnp.sum(kw, axis=0, keepdims=True)
            m_ref[0] = jnp.where(lane_id == h, m_new, m_all)
        hh = jnp.concatenate(hh_parts, axis=0) if len(hh_parts) > 1 else hh_parts[0]
        o_gate = _sigmoid(proj(wm_ref, bm_ref, 3 * mw + h * hdim, hdim))
        ha_parts.append(_head_norm(hh, mng_ref[:, hs]) * o_gate)
    ha = jnp.concatenate(ha_parts, axis=1)

    pre = proj(wc_ref, bc_ref, width, width) * proj(wc_ref, bc_ref, 2 * width, width)
    pre_ref[SUBLANE:SUBLANE + rows, :] = pre
    conv = convb_ref[...] + pre * convw_ref[CONV_W - 1:CONV_W, :]
    for j in range(CONV_W - 1):
        conv = conv + pre_ref[carry0 + j:carry0 + j + rows, :] * convw_ref[j:j + 1, :]
    tail = pre_ref[rows + carry0:rows + SUBLANE, :]
    pre_ref[carry0:SUBLANE, :] = tail
    conv_ref[0] = tail
    hb = proj(wc_ref, bc_ref, 0, width) * conv

    o_ref[0] = _merge_tail(x, ha, hb, proj(wc_ref, bc_ref, 3 * width, d), proj(wc_ref, bc_ref, 3 * width + d, d),
                           wa_ref, wb_ref, wmix_ref, lng_ref, lnb_ref, alpha)


def _prompt_mix_call(x, consts, *, alpha, heads, hdim, width):
    nb, seq, d = x.shape
    chunk = min(MLSTM_CHUNK, seq)
    rows = min(MIX_ROWS, seq)
    assert seq % rows == 0 and rows % chunk == 0 and chunk % LANE == 0
    kern = functools.partial(_prompt_mix_kernel, alpha=alpha, heads=heads, hdim=hdim, width=width, chunk=chunk)
    out_shape = (
        jax.ShapeDtypeStruct((nb, seq, d), F32),
        jax.ShapeDtypeStruct((nb, heads, hdim, hdim), F32),
        jax.ShapeDtypeStruct((nb, 1, heads * hdim), F32),
        jax.ShapeDtypeStruct((nb, 1, LANE), F32),
        jax.ShapeDtypeStruct((nb, CONV_W - 1, width), F32),
    )
    out_specs = (
        pl.BlockSpec((1, rows, d), lambda b, t: (b, t, 0)),
        pl.BlockSpec((1, heads, hdim, hdim), lambda b, t: (b, 0, 0, 0)),
        pl.BlockSpec((1, 1, heads * hdim), lambda b, t: (b, 0, 0)),
        pl.BlockSpec((1, 1, LANE), lambda b, t: (b, 0, 0)),
        pl.BlockSpec((1, CONV_W - 1, width), lambda b, t: (b, 0, 0)),
    )
    return pl.pallas_call(
        kern,
        grid=(nb, seq // rows),
        in_specs=[pl.BlockSpec((1, rows, d), lambda b, t: (b, t, 0))] + [_const_spec(a.shape) for a in consts],
        out_specs=out_specs,
        out_shape=out_shape,
        scratch_shapes=[pltpu.VMEM((rows + SUBLANE, width), F32)],
        compiler_params=pltpu.CompilerParams(dimension_semantics=("arbitrary", "arbitrary"),
                                             vmem_limit_bytes=V7X_VMEM_LIMIT_BYTES),
        name="prompt_mix",
    )(x, *consts)


def _proj_kernel(x_ref, wm_ref, wg_ref, wc_ref, bm_ref, bg_ref, bc_ref, o_ref):
    xb = x_ref[...].astype(BF16)
    c0 = 0
    for w_ref, b_ref in ((wm_ref, bm_ref), (wg_ref, bg_ref), (wc_ref, bc_ref)):
        w = w_ref.shape[1]
        o_ref[:, c0:c0 + w] = jnp.dot(xb, w_ref[...], preferred_element_type=F32) + b_ref[...]
        c0 += w


def _proj_call(x, wm, wg, wc, bm, bg, bc):
    n = x.shape[0]
    return pl.pallas_call(
        _proj_kernel,
        out_shape=jax.ShapeDtypeStruct((n, wm.shape[1] + wg.shape[1] + wc.shape[1]), F32),
        compiler_params=pltpu.CompilerParams(vmem_limit_bytes=V7X_VMEM_LIMIT_BYTES),
        name="sample_proj",
    )(x, wm, wg, wc, bm, bg, bc)


def _sample_step_kernel(z_ref, c_ref, n_ref, m_ref, buf_ref, mng_ref, convw_ref, convb_ref,
                        ha_ref, hb_ref, co_ref, no_ref, mo_ref, bo_ref, num_ref,
                        *, heads, hdim, width):
    bb = z_ref.shape[0]
    mw = heads * hdim
    cb0 = 4 * mw + LANE
    zg = z_ref[:, 4 * mw:4 * mw + LANE]
    lane_id = lax.broadcasted_iota(jnp.int32, (bb, LANE), 1)
    m_out = jnp.zeros((bb, LANE), F32)
    reps = LANE // bb
    for h in range(heads):
        hs = slice(h * hdim, (h + 1) * hdim)
        q = z_ref[:, hs]
        k = z_ref[:, mw + h * hdim:mw + (h + 1) * hdim] * (hdim ** -0.5)
        v = z_ref[:, 2 * mw + h * hdim:2 * mw + (h + 1) * hdim]
        li = zg[:, h:h + 1]
        lf = _log_sigmoid(zg[:, heads + h:heads + h + 1])
        m_prev = m_ref[:, h:h + 1]
        m_new = jnp.maximum(lf + m_prev, li)
        w_c = jnp.exp(lf + m_prev - m_new)
        kw = k * jnp.exp(li - m_new)
        n_new = w_c * n_ref[:, hs] + kw
        no_ref[:, hs] = n_new
        m_out = jnp.where(lane_id == h, m_new, m_out)
        q_t = jnp.concatenate([q] * reps, axis=0).T
        kw_t = jnp.concatenate([kw] * reps, axis=0).T
        for j in range(bb):
            c_new = w_c[j:j + 1, :] * c_ref[j, h] + kw_t[:, j:j + 1] * v[j:j + 1, :]
            co_ref[j, h] = c_new
            num_ref[j:j + 1, hs] = jnp.sum(q_t[:, j:j + 1] * c_new, axis=0, keepdims=True)
        den = jnp.sum(q * n_new, axis=1, keepdims=True)
        hh = num_ref[:, hs] * (1.0 / jnp.maximum(jnp.abs(den), jnp.exp(-m_new)))
        o_gate = _sigmoid(z_ref[:, 3 * mw + h * hdim:3 * mw + (h + 1) * hdim])
        ha_ref[:, hs] = _head_norm(hh, mng_ref[:, hs]) * o_gate
    mo_ref[...] = m_out
    pre = z_ref[:, cb0 + width:cb0 + 2 * width] * z_ref[:, cb0 + 2 * width:cb0 + 3 * width]
    conv = convb_ref[...] + pre * convw_ref[CONV_W - 1:CONV_W, :]
    for j in range(CONV_W - 1):
        conv = conv + buf_ref[:, j * width:(j + 1) * width] * convw_ref[j:j + 1, :]
    for j in range(CONV_W - 2):
        bo_ref[:, j * width:(j + 1) * width] = buf_ref[:, (j + 1) * width:(j + 2) * width]
    bo_ref[:, (CONV_W - 2) * width:(CONV_W - 1) * width] = pre
    hb_ref[...] = z_ref[:, cb0:cb0 + width] * conv


def _sample_step_call(z, c0, n0, m0, buf0, mng, convw, convb, *, heads, hdim):
    n = z.shape[0]
    bb = SUBLANE
    assert n % bb == 0
    mw = heads * hdim
    width = convw.shape[1]
    kern = functools.partial(_sample_step_kernel, heads=heads, hdim=hdim, width=width)
    row = lambda w: pl.BlockSpec((bb, w), lambda i: (i, 0))
    c_spec = pl.BlockSpec((bb, heads, hdim, hdim), lambda i: (i, 0, 0, 0))
    out_shape = (
        jax.ShapeDtypeStruct((n, mw), F32), jax.ShapeDtypeStruct((n, width), F32),
        jax.ShapeDtypeStruct(c0.shape, F32), jax.ShapeDtypeStruct((n, mw), F32),
        jax.ShapeDtypeStruct((n, LANE), F32), jax.ShapeDtypeStruct(buf0.shape, F32),
    )
    return pl.pallas_call(
        kern,
        grid=(n // bb,),
        in_specs=[row(z.shape[1]), c_spec, row(mw), row(heads), row(buf0.shape[1]),
                  _const_spec(mng.shape), _const_spec(convw.shape), _const_spec(convb.shape)],
        out_specs=(row(mw), row(width), c_spec, row(mw), row(LANE), row(buf0.shape[1])),
        out_shape=out_shape,
        scratch_shapes=[pltpu.VMEM((bb, mw), F32)],
        compiler_params=pltpu.CompilerParams(dimension_semantics=("arbitrary",),
                                             vmem_limit_bytes=V7X_VMEM_LIMIT_BYTES),
        name="sample_step",
    )(z, c0, n0, m0, buf0, mng, convw, convb)


def _sample_merge_kernel(x_ref, ha_ref, hb_ref, z_ref, wa_ref, wb_ref, wmix_ref, lng_ref, lnb_ref, o_ref,
                         *, alpha, ga0):
    d = x_ref.shape[1]
    o_ref[...] = _merge_tail(x_ref[...], ha_ref[...], hb_ref[...], z_ref[:, ga0:ga0 + d],
                             z_ref[:, ga0 + d:ga0 + 2 * d], wa_ref, wb_ref, wmix_ref, lng_ref, lnb_ref, alpha)


def _sample_merge_call(x, ha, hb, z, wa, wb, wmix, lng, lnb, *, alpha, ga0):
    kern = functools.partial(_sample_merge_kernel, alpha=alpha, ga0=ga0)
    return pl.pallas_call(
        kern,
        out_shape=jax.ShapeDtypeStruct(x.shape, F32),
        compiler_params=pltpu.CompilerParams(vmem_limit_bytes=V7X_VMEM_LIMIT_BYTES),
        name="sample_merge",
    )(x, ha, hb, z, wa, wb, wmix, lng, lnb)


def kernel(x_prompt, x_sample, p_prompt, p_sample, state_C, state_n, state_m, state_conv, w_in, b_in, m_norm_g, w_a, w_b, conv_w, conv_b, w_mix, ffn1_wi, ffn1_wo, ffn2_wi, ffn2_wo, w_pg, w_pp, ln_g, ln_b):
    depth = w_in.shape[0]
    nb, seq, d = x_prompt.shape
    ns = x_sample.shape[0]
    assert x_sample.shape[1] == 1
    heads, hdim = m_norm_g.shape[1], m_norm_g.shape[2]
    mw = heads * hdim
    width = conv_w.shape[2]
    ngate = 2 * heads
    assert ngate <= SUBLANE
    alpha = (2 * depth) ** 0.25
    gate0 = 4 * mw

    xp = x_prompt
    xs = x_sample.reshape(ns, d)
    outs = [[] for _ in range(8)]
    for i in range(depth):
        wi_l, bi_l = w_in[i], b_in[i]
        w_gates = wi_l[:, gate0:gate0 + ngate]
        b_gates = bi_l[gate0:gate0 + ngate]
        proj_consts = (
            wi_l[:, :gate0].astype(BF16),
            jnp.pad(w_gates, ((0, 0), (0, LANE - ngate))).astype(BF16),
            wi_l[:, gate0 + ngate:].astype(BF16),
            bi_l[None, :gate0],
            jnp.pad(b_gates, (0, LANE - ngate))[None, :],
            bi_l[None, gate0 + ngate:],
        )
        gates_t = (
            jnp.pad(w_gates.T, ((0, SUBLANE - ngate), (0, 0))).astype(BF16),
            jnp.broadcast_to(jnp.pad(b_gates, (0, SUBLANE - ngate))[:, None], (SUBLANE, LANE)),
        )
        mng = m_norm_g[i].reshape(1, mw)
        convw, convb = conv_w[i], conv_b[i][None, :]
        wa, wb, wmix = w_a[i].astype(BF16), w_b[i].astype(BF16), w_mix[i].astype(BF16)
        wpg, wpp = w_pg[i].astype(BF16), w_pp[i].astype(BF16)
        f1 = (ffn1_wi[i].astype(BF16), ffn1_wo[i].astype(BF16))
        f2 = (ffn2_wi[i].astype(BF16), ffn2_wo[i].astype(BF16))
        lng, lnb = ln_g[i], ln_b[i]
        ga0 = gate0 + LANE + 3 * width
        mix_consts = proj_consts + gates_t + (mng, convw, convb, wa, wb, wmix, lng, lnb)

        x1 = _ffn_call(xp.reshape(nb * seq, d), None, *f1, None, None, lng, lnb, alpha=alpha, ln_row=0,
                       with_embed=False, name="prompt_ffn1")
        x2, c_p, n_p, m_p, buf_p = _prompt_mix_call(x1.reshape(nb, seq, d), mix_consts, alpha=alpha, heads=heads,
                                                    hdim=hdim, width=width)
        xp = _ffn_call(x2.reshape(nb * seq, d), p_prompt[i].reshape(nb * seq, -1), *f2, wpg, wpp, lng, lnb,
                       alpha=alpha, ln_row=2, with_embed=True, name="prompt_ffn2").reshape(nb, seq, d)

        s1 = _ffn_call(xs, None, *f1, None, None, lng, lnb, alpha=alpha, ln_row=0, with_embed=False,
                       name="sample_ffn1")
        z = _proj_call(s1, *proj_consts)
        ha, hb, c_s, n_s, m_s, buf_s = _sample_step_call(
            z, state_C[i], state_n[i].reshape(ns, mw), state_m[i], state_conv[i].reshape(ns, -1),
            mng, convw, convb, heads=heads, hdim=hdim)
        s2 = _sample_merge_call(s1, ha, hb, z, wa, wb, wmix, lng, lnb, alpha=alpha, ga0=ga0)
        xs = _ffn_call(s2, p_sample[i].reshape(ns, -1), *f2, wpg, wpp, lng, lnb, alpha=alpha, ln_row=2,
                       with_embed=True, name="sample_ffn2")

        for lst, val in zip(outs, (c_p, n_p.reshape(nb, heads, hdim), m_p[:, 0, :heads], buf_p,
                                   c_s, n_s.reshape(ns, heads, hdim), m_s[:, :heads],
                                   buf_s.reshape(ns, CONV_W - 1, width))):
            lst.append(val)
    return (xp, xs.reshape(ns, 1, d)) + tuple(jnp.stack(o) for o in outs)
```

```python
import functools

import jax
import jax.numpy as jnp
from jax import lax
from jax.experimental import pallas as pl
from jax.experimental.pallas import tpu as pltpu

F32 = jnp.float32
BF16 = jnp.bfloat16

LN_EPS = 1e-5
CONV_W = 3
LANE = 128
SUBLANE = 8
V7X_VMEM_LIMIT_BYTES = 60 * 1024 * 1024
MLSTM_CHUNK = 256
MIX_ROWS = 512
FFN1_ROWS = 512
FFN2_ROWS = 1024
FFN_COLS = 256


def _dot(a, b):
    return jnp.dot(a.astype(BF16), b.astype(BF16), preferred_element_type=F32)


def _dot_nt(a, b):
    return lax.dot_general(a.astype(BF16), b.astype(BF16), (((1,), (1,)), ((), ())),
                           preferred_element_type=F32)


def _dot_tn(a, b):
    return lax.dot_general(a.astype(BF16), b.astype(BF16), (((0,), (0,)), ((), ())),
                           preferred_element_type=F32)


def _split3(x):
    hi = x.astype(BF16)
    r1 = x - hi.astype(F32)
    mid = r1.astype(BF16)
    lo = (r1 - mid.astype(F32)).astype(BF16)
    return hi, mid, lo


def _layer_norm(x, g, b):
    mu = jnp.mean(x, axis=-1, keepdims=True)
    xc = x - mu
    var = jnp.mean(xc * xc, axis=-1, keepdims=True)
    return xc * lax.rsqrt(var + LN_EPS) * g + b


def _head_norm(h, g):
    mu = jnp.mean(h, axis=-1, keepdims=True)
    hc = h - mu
    var = jnp.mean(hc * hc, axis=-1, keepdims=True)
    return hc * lax.rsqrt(var + LN_EPS) * g


def _sigmoid(x):
    return 1.0 / (1.0 + jnp.exp(-x))


def _log_sigmoid(x):
    return jnp.minimum(x, 0.0) - jnp.log1p(jnp.exp(-jnp.abs(x)))


def _const_spec(shape):
    nd = len(shape)
    return pl.BlockSpec(shape, lambda *_: (0,) * nd, pipeline_mode=pl.Buffered(1))


def _cast_in_proj_rows(win_ref, wm_out, wg_out, wc_out, ngate):
    gate0 = wm_out.shape[1]
    wm_out[...] = win_ref[:, :gate0].astype(BF16)
    g = win_ref[:, gate0:gate0 + LANE]
    lane_id = lax.broadcasted_iota(jnp.int32, g.shape, 1)
    wg_out[...] = jnp.where(lane_id < ngate, g, 0.0).astype(BF16)
    wc_out[...] = win_ref[:, gate0:][:, ngate:].astype(BF16)


def _ffn_kernel(*refs, alpha, ln_row, with_embed, ngate):
    n_in = 5 + (3 if with_embed else 0) + (4 if ngate else 0)
    ins, outs = refs[:n_in], refs[n_in:]
    x_ref, ins = ins[0], ins[1:]
    if with_embed:
        p_ref, wi_ref, wo_ref, wpg_ref, wpp_ref, lng_ref, lnb_ref = ins[:7]
    else:
        wi_ref, wo_ref, lng_ref, lnb_ref = ins[:4]
    o_ref = outs[0]
    if ngate:
        win_ref, wa_ref, wb_ref, wmix_ref = ins[-4:]
        wm_out, wg_out, wc_out, wa_out, wb_out, wmix_out = outs[1:]
        _cast_in_proj_rows(win_ref, wm_out, wg_out, wc_out, ngate)
        for src, dst in ((wa_ref, wa_out), (wb_ref, wb_out), (wmix_ref, wmix_out)):
            dst[...] = src[...].astype(BF16)
    d_ff = wo_ref.shape[0]
    x = x_ref[...]
    xb = x.astype(BF16)
    acc = jnp.zeros(x.shape, F32)
    for c0 in range(0, d_ff, FFN_COLS):
        g = jnp.dot(xb, wi_ref[:, c0:c0 + FFN_COLS].astype(BF16), preferred_element_type=F32)
        u = jnp.dot(xb, wi_ref[:, d_ff + c0:d_ff + c0 + FFN_COLS].astype(BF16), preferred_element_type=F32)
        a = (g * _sigmoid(g)) * u
        acc = acc + jnp.dot(a.astype(BF16), wo_ref[c0:c0 + FFN_COLS, :].astype(BF16),
                            preferred_element_type=F32)
    y = _layer_norm(alpha * x + 0.5 * acc, lng_ref[ln_row:ln_row + 1, :], lnb_ref[ln_row:ln_row + 1, :])
    if with_embed:
        gate = _sigmoid(jnp.dot(y.astype(BF16), wpg_ref[...], preferred_element_type=F32))
        emb = jnp.dot(p_ref[...].astype(BF16), wpp_ref[...], preferred_element_type=F32)
        y = _layer_norm(alpha * y + gate * emb, lng_ref[ln_row + 1:ln_row + 2, :],
                        lnb_ref[ln_row + 1:ln_row + 2, :])
    o_ref[...] = y


def _ffn_call(x, p, wi, wo, wpg, wpp, lng, lnb, *, alpha, ln_row, with_embed, name, rows, mix_weights=None,
              gate0=0, ngate=0):
    n, d = x.shape
    rows = min(rows, n)
    steps = n // rows
    assert n % rows == 0 and wo.shape[0] % FFN_COLS == 0
    row_spec = lambda r, w: pl.BlockSpec((r, w), lambda i: (i, 0))
    if with_embed:
        args = (x, p, wi, wo, wpg, wpp, lng, lnb)
        in_specs = [row_spec(rows, d), row_spec(rows, p.shape[1])] + [_const_spec(a.shape) for a in args[2:]]
    else:
        args = (x, wi, wo, lng, lnb)
        in_specs = [row_spec(rows, d)] + [_const_spec(a.shape) for a in args[1:]]
    out_specs = [row_spec(rows, d)]
    out_shape = [jax.ShapeDtypeStruct((n, d), F32)]
    if mix_weights is not None:
        w_in = mix_weights[0]
        k_in, n_in = w_in.shape
        wr = k_in // steps
        assert k_in % steps == 0 and wr % (2 * SUBLANE) == 0 and ngate > 0 and gate0 % LANE == 0
        args = args + tuple(mix_weights)
        in_specs += [row_spec(wr, a.shape[1]) for a in mix_weights]
        side_widths = (gate0, LANE, n_in - gate0 - ngate) + tuple(a.shape[1] for a in mix_weights[1:])
        out_specs += [row_spec(wr, w) for w in side_widths]
        out_shape += [jax.ShapeDtypeStruct((k_in, w), BF16) for w in side_widths]
    kern = functools.partial(_ffn_kernel, alpha=alpha, ln_row=ln_row, with_embed=with_embed,
                             ngate=ngate if mix_weights is not None else 0)
    res = pl.pallas_call(
        kern,
        grid=(steps,),
        in_specs=in_specs,
        out_specs=tuple(out_specs),
        out_shape=tuple(out_shape),
        compiler_params=pltpu.CompilerParams(dimension_semantics=("arbitrary",),
                                             vmem_limit_bytes=V7X_VMEM_LIMIT_BYTES),
        name=name,
    )(*args)
    return res[0] if mix_weights is None else res


def _merge_tail(x, ha, hb, z_ga, z_gb, wa_ref, wb_ref, wmix_ref, lng_ref, lnb_ref, alpha):
    y_a = jnp.dot(ha.astype(BF16), wa_ref[...], preferred_element_type=F32)
    y_b = jnp.dot(hb.astype(BF16), wb_ref[...], preferred_element_type=F32)
    merged = _sigmoid(z_ga) * y_a + _sigmoid(z_gb) * y_b
    mix = jnp.dot(merged.astype(BF16), wmix_ref[...], preferred_element_type=F32)
    return _layer_norm(alpha * x + mix, lng_ref[1:2, :], lnb_ref[1:2, :])


def _prompt_mix_kernel(x_ref, wm_ref, wg_ref, wc_ref, bm_ref, bg_ref, bc_ref, wgt_ref, bgt_ref,
                       mng_ref, convw_ref, convb_ref, wa_ref, wb_ref, wmix_ref, lng_ref, lnb_ref,
                       *rest, alpha, heads, hdim, width, chunk, cast_blocks):
    n_cast = len(cast_blocks)
    cast_in, rest = rest[:n_cast], rest[n_cast:]
    o_ref, c_ref, n_ref, m_ref, conv_ref = rest[:5]
    cast_out, pre_ref = rest[5:5 + n_cast], rest[5 + n_cast]
    t = pl.program_id(1)
    rows, d = x_ref.shape[1], x_ref.shape[2]
    mw = heads * hdim
    carry0 = SUBLANE - (CONV_W - 1)

    @pl.when(t == 0)
    def _():
        c_ref[...] = jnp.zeros(c_ref.shape, F32)
        n_ref[...] = jnp.zeros(n_ref.shape, F32)
        m_ref[...] = jnp.zeros(m_ref.shape, F32)
        pre_ref[0:SUBLANE, :] = jnp.zeros((SUBLANE, width), F32)

    for src, dst, k in zip(cast_in, cast_out, cast_blocks):
        @pl.when(t < k)
        def _(src=src, dst=dst):
            dst[...] = src[...].astype(BF16)

    x = x_ref[0]
    xb = x.astype(BF16)

    def proj(w_ref, b_ref, c0, w):
        return jnp.dot(xb, w_ref[:, c0:c0 + w], preferred_element_type=F32) + b_ref[:, c0:c0 + w]

    gcol = proj(wg_ref, bg_ref, 0, LANE)
    grow = _dot_nt(wgt_ref[...], xb) + bgt_ref[:, 0:1]
    lf_col = _log_sigmoid(gcol)
    lf_row = _log_sigmoid(grow)
    r_id = lax.broadcasted_iota(jnp.int32, (chunk, chunk), 0)
    c_id = lax.broadcasted_iota(jnp.int32, (chunk, chunk), 1)
    causal = c_id <= r_id
    lower = causal.astype(BF16)
    upper = (r_id <= c_id).astype(BF16)
    lane_id = lax.broadcasted_iota(jnp.int32, (1, LANE), 1)
    starts = list(range(0, rows, chunk))
    b_cols = [sum(jnp.dot(lower, part, preferred_element_type=F32) for part in _split3(lf_col[r0:r0 + chunk]))
              for r0 in starts]
    b_rows = [sum(jnp.dot(part, upper, preferred_element_type=F32) for part in _split3(lf_row[:, r0:r0 + chunk]))
              for r0 in starts]

    ha_parts = []
    for h in range(heads):
        hs = slice(h * hdim, (h + 1) * hdim)
        q_all = proj(wm_ref, bm_ref, h * hdim, hdim)
        k_all = proj(wm_ref, bm_ref, mw + h * hdim, hdim) * (hdim ** -0.5)
        v_all = proj(wm_ref, bm_ref, 2 * mw + h * hdim, hdim)
        hh_parts = []
        for ci, r0 in enumerate(starts):
            rs = slice(r0, r0 + chunk)
            q, k, v = q_all[rs], k_all[rs], v_all[rs]
            b_col = b_cols[ci][:, heads + h:heads + h + 1]
            b_row = b_rows[ci][heads + h:heads + h + 1, :]
            li_c = gcol[rs, h:h + 1]
            li_r = grow[h:h + 1, rs]
            m_all = m_ref[0]
            m_prev = m_all[:, h:h + 1]
            c_h = c_ref[0, h]
            n_h = n_ref[0, :, hs]

            dmat = jnp.where(causal, b_col - b_row + li_r, -jnp.inf)
            inter = b_col + m_prev
            m_row = jnp.maximum(inter, jnp.max(dmat, axis=1, keepdims=True))
            w_inter = jnp.exp(inter - m_row)
            s = _dot_nt(q, k) * jnp.exp(dmat - m_row)
            num = w_inter * _dot(q, c_h) + _dot(s, v)
            den = w_inter * jnp.sum(q * n_h, axis=1, keepdims=True) + jnp.sum(s, axis=1, keepdims=True)
            hh_parts.append(num * (1.0 / jnp.maximum(jnp.abs(den), jnp.exp(-m_row))))

            b_end = b_col[chunk - 1:chunk, :]
            dec = b_end - b_col + li_c
            m_new = jnp.maximum(b_end + m_prev, jnp.max(dec, axis=0, keepdims=True))
            w_c = jnp.exp(b_end + m_prev - m_new)
            kw = k * jnp.exp(dec - m_new)
            c_ref[0, h] = w_c * c_h + _dot_tn(kw, v)
            n_ref[0, :, hs] = w_c * n_h + jnp.sum(kw, axis=0, keepdims=True)
            m_ref[0] = jnp.where(lane_id == h, m_new, m_all)
        hh = jnp.concatenate(hh_parts, axis=0) if len(hh_parts) > 1 else hh_parts[0]
        o_gate = _sigmoid(proj(wm_ref, bm_ref, 3 * mw + h * hdim, hdim))
        ha_parts.append(_head_norm(hh, mng_ref[:, hs]) * o_gate)
    ha = jnp.concatenate(ha_parts, axis=1)

    pre = proj(wc_ref, bc_ref, width, width) * proj(wc_ref, bc_ref, 2 * width, width)
    pre_ref[SUBLANE:SUBLANE + rows, :] = pre
    conv = convb_ref[...] + pre * convw_ref[CONV_W - 1:CONV_W, :]
    for j in range(CONV_W - 1):
        conv = conv + pre_ref[carry0 + j:carry0 + j + rows, :] * convw_ref[j:j + 1, :]
    tail = pre_ref[rows + carry0:rows + SUBLANE, :]
    pre_ref[carry0:SUBLANE, :] = tail
    conv_ref[0] = tail
    hb = proj(wc_ref, bc_ref, 0, width) * conv

    o_ref[0] = _merge_tail(x, ha, hb, proj(wc_ref, bc_ref, 3 * width, d), proj(wc_ref, bc_ref, 3 * width + d, d),
                           wa_ref, wb_ref, wmix_ref, lng_ref, lnb_ref, alpha)


def _cast_blocks_per_seq(n_rows, nb, steps):
    for k in range(steps, 0, -1):
        if n_rows % (nb * k) == 0 and (n_rows // (nb * k)) % (2 * SUBLANE) == 0:
            return k
    raise ValueError(f"cannot split {n_rows} weight rows over {nb} sequences")


def _prompt_mix_call(x, consts, cast_weights, *, alpha, heads, hdim, width):
    nb, seq, d = x.shape
    chunk = min(MLSTM_CHUNK, seq)
    rows = min(MIX_ROWS, seq)
    steps = seq // rows
    assert seq % rows == 0 and rows % chunk == 0 and chunk % LANE == 0
    cast_blocks = tuple(_cast_blocks_per_seq(w.shape[0], nb, steps) for w in cast_weights)
    kern = functools.partial(_prompt_mix_kernel, alpha=alpha, heads=heads, hdim=hdim, width=width, chunk=chunk,
                             cast_blocks=cast_blocks)
    out_shape = (
        jax.ShapeDtypeStruct((nb, seq, d), F32),
        jax.ShapeDtypeStruct((nb, heads, hdim, hdim), F32),
        jax.ShapeDtypeStruct((nb, 1, heads * hdim), F32),
        jax.ShapeDtypeStruct((nb, 1, LANE), F32),
        jax.ShapeDtypeStruct((nb, CONV_W - 1, width), F32),
    ) + tuple(jax.ShapeDtypeStruct(w.shape, BF16) for w in cast_weights)

    def cast_spec(w, k):
        return pl.BlockSpec((w.shape[0] // (nb * k), w.shape[1]), lambda b, t: (b * k + jnp.minimum(t, k - 1), 0))

    cast_specs = tuple(cast_spec(w, k) for w, k in zip(cast_weights, cast_blocks))
    out_specs = (
        pl.BlockSpec((1, rows, d), lambda b, t: (b, t, 0)),
        pl.BlockSpec((1, heads, hdim, hdim), lambda b, t: (b, 0, 0, 0)),
        pl.BlockSpec((1, 1, heads * hdim), lambda b, t: (b, 0, 0)),
        pl.BlockSpec((1, 1, LANE), lambda b, t: (b, 0, 0)),
        pl.BlockSpec((1, CONV_W - 1, width), lambda b, t: (b, 0, 0)),
    ) + cast_specs
    res = pl.pallas_call(
        kern,
        grid=(nb, steps),
        in_specs=[pl.BlockSpec((1, rows, d), lambda b, t: (b, t, 0))] + [_const_spec(a.shape) for a in consts]
        + list(cast_specs),
        out_specs=out_specs,
        out_shape=out_shape,
        scratch_shapes=[pltpu.VMEM((rows + SUBLANE, width), F32)],
        compiler_params=pltpu.CompilerParams(dimension_semantics=("arbitrary", "arbitrary"),
                                             vmem_limit_bytes=V7X_VMEM_LIMIT_BYTES),
        name="prompt_mix",
    )(x, *consts, *cast_weights)
    return res[:5], res[5:]


def _proj_kernel(x_ref, wm_ref, wg_ref, wc_ref, bm_ref, bg_ref, bc_ref, o_ref):
    xb = x_ref[...].astype(BF16)
    c0 = 0
    for w_ref, b_ref in ((wm_ref, bm_ref), (wg_ref, bg_ref), (wc_ref, bc_ref)):
        w = w_ref.shape[1]
        o_ref[:, c0:c0 + w] = jnp.dot(xb, w_ref[...], preferred_element_type=F32) + b_ref[...]
        c0 += w


def _proj_call(x, wm, wg, wc, bm, bg, bc):
    n = x.shape[0]
    return pl.pallas_call(
        _proj_kernel,
        out_shape=jax.ShapeDtypeStruct((n, wm.shape[1] + wg.shape[1] + wc.shape[1]), F32),
        compiler_params=pltpu.CompilerParams(vmem_limit_bytes=V7X_VMEM_LIMIT_BYTES),
        name="sample_proj",
    )(x, wm, wg, wc, bm, bg, bc)


def _sample_step_kernel(z_ref, c_ref, n_ref, m_ref, buf_ref, mng_ref, convw_ref, convb_ref,
                        ha_ref, hb_ref, co_ref, no_ref, mo_ref, bo_ref, num_ref,
                        *, heads, hdim, width):
    bb = z_ref.shape[0]
    mw = heads * hdim
    cb0 = 4 * mw + LANE
    zg = z_ref[:, 4 * mw:4 * mw + LANE]
    lane_id = lax.broadcasted_iota(jnp.int32, (bb, LANE), 1)
    m_out = jnp.zeros((bb, LANE), F32)
    reps = LANE // bb
    for h in range(heads):
        hs = slice(h * hdim, (h + 1) * hdim)
        q = z_ref[:, hs]
        k = z_ref[:, mw + h * hdim:mw + (h + 1) * hdim] * (hdim ** -0.5)
        v = z_ref[:, 2 * mw + h * hdim:2 * mw + (h + 1) * hdim]
        li = zg[:, h:h + 1]
        lf = _log_sigmoid(zg[:, heads + h:heads + h + 1])
        m_prev = m_ref[:, h:h + 1]
        m_new = jnp.maximum(lf + m_prev, li)
        w_c = jnp.exp(lf + m_prev - m_new)
        kw = k * jnp.exp(li - m_new)
        n_new = w_c * n_ref[:, hs] + kw
        no_ref[:, hs] = n_new
        m_out = jnp.where(lane_id == h, m_new, m_out)
        q_t = jnp.concatenate([q] * reps, axis=0).T
        kw_t = jnp.concatenate([kw] * reps, axis=0).T
        for j in range(bb):
            c_new = w_c[j:j + 1, :] * c_ref[j, h] + kw_t[:, j:j + 1] * v[j:j + 1, :]
            co_ref[j, h] = c_new
            num_ref[j:j + 1, hs] = jnp.sum(q_t[:, j:j + 1] * c_new, axis=0, keepdims=True)
        den = jnp.sum(q * n_new, axis=1, keepdims=True)
        hh = num_ref[:, hs] * (1.0 / jnp.maximum(jnp.abs(den), jnp.exp(-m_new)))
        o_gate = _sigmoid(z_ref[:, 3 * mw + h * hdim:3 * mw + (h + 1) * hdim])
        ha_ref[:, hs] = _head_norm(hh, mng_ref[:, hs]) * o_gate
    mo_ref[...] = m_out
    pre = z_ref[:, cb0 + width:cb0 + 2 * width] * z_ref[:, cb0 + 2 * width:cb0 + 3 * width]
    conv = convb_ref[...] + pre * convw_ref[CONV_W - 1:CONV_W, :]
    for j in range(CONV_W - 1):
        conv = conv + buf_ref[:, j * width:(j + 1) * width] * convw_ref[j:j + 1, :]
    for j in range(CONV_W - 2):
        bo_ref[:, j * width:(j + 1) * width] = buf_ref[:, (j + 1) * width:(j + 2) * width]
    bo_ref[:, (CONV_W - 2) * width:(CONV_W - 1) * width] = pre
    hb_ref[...] = z_ref[:, cb0:cb0 + width] * conv


def _sample_step_call(z, c0, n0, m0, buf0, mng, convw, convb, *, heads, hdim):
    n = z.shape[0]
    bb = SUBLANE
    assert n % bb == 0
    mw = heads * hdim
    width = convw.shape[1]
    kern = functools.partial(_sample_step_kernel, heads=heads, hdim=hdim, width=width)
    row = lambda w: pl.BlockSpec((bb, w), lambda i: (i, 0))
    c_spec = pl.BlockSpec((bb, heads, hdim, hdim), lambda i: (i, 0, 0, 0))
    out_shape = (
        jax.ShapeDtypeStruct((n, mw), F32), jax.ShapeDtypeStruct((n, width), F32),
        jax.ShapeDtypeStruct(c0.shape, F32), jax.ShapeDtypeStruct((n, mw), F32),
        jax.ShapeDtypeStruct((n, LANE), F32), jax.ShapeDtypeStruct(buf0.shape, F32),
    )
    return pl.pallas_call(
        kern,
        grid=(n // bb,),
        in_specs=[row(z.shape[1]), c_spec, row(mw), row(heads), row(buf0.shape[1]),
                  _const_spec(mng.shape), _const_spec(convw.shape), _const_spec(convb.shape)],
        out_specs=(row(mw), row(width), c_spec, row(mw), row(LANE), row(buf0.shape[1])),
        out_shape=out_shape,
        scratch_shapes=[pltpu.VMEM((bb, mw), F32)],
        compiler_params=pltpu.CompilerParams(dimension_semantics=("arbitrary",),
                                             vmem_limit_bytes=V7X_VMEM_LIMIT_BYTES),
        name="sample_step",
    )(z, c0, n0, m0, buf0, mng, convw, convb)


def _sample_merge_kernel(x_ref, ha_ref, hb_ref, z_ref, wa_ref, wb_ref, wmix_ref, lng_ref, lnb_ref, o_ref,
                         *, alpha, ga0):
    d = x_ref.shape[1]
    o_ref[...] = _merge_tail(x_ref[...], ha_ref[...], hb_ref[...], z_ref[:, ga0:ga0 + d],
                             z_ref[:, ga0 + d:ga0 + 2 * d], wa_ref, wb_ref, wmix_ref, lng_ref, lnb_ref, alpha)


def _sample_merge_call(x, ha, hb, z, wa, wb, wmix, lng, lnb, *, alpha, ga0):
    kern = functools.partial(_sample_merge_kernel, alpha=alpha, ga0=ga0)
    return pl.pallas_call(
        kern,
        out_shape=jax.ShapeDtypeStruct(x.shape, F32),
        compiler_params=pltpu.CompilerParams(vmem_limit_bytes=V7X_VMEM_LIMIT_BYTES),
        name="sample_merge",
    )(x, ha, hb, z, wa, wb, wmix, lng, lnb)


def kernel(x_prompt, x_sample, p_prompt, p_sample, state_C, state_n, state_m, state_conv, w_in, b_in, m_norm_g, w_a, w_b, conv_w, conv_b, w_mix, ffn1_wi, ffn1_wo, ffn2_wi, ffn2_wo, w_pg, w_pp, ln_g, ln_b):
    depth = w_in.shape[0]
    nb, seq, d = x_prompt.shape
    ns = x_sample.shape[0]
    assert x_sample.shape[1] == 1
    heads, hdim = m_norm_g.shape[1], m_norm_g.shape[2]
    mw = heads * hdim
    width = conv_w.shape[2]
    ngate = 2 * heads
    assert ngate <= SUBLANE
    alpha = (2 * depth) ** 0.25
    gate0 = 4 * mw

    xp = x_prompt
    xs = x_sample.reshape(ns, d)
    outs = [[] for _ in range(8)]
    for i in range(depth):
        wi_l, bi_l = w_in[i], b_in[i]
        w_gates = wi_l[:, gate0:gate0 + ngate]
        b_gates = bi_l[gate0:gate0 + ngate]
        proj_biases = (bi_l[None, :gate0], jnp.pad(b_gates, (0, LANE - ngate))[None, :], bi_l[None, gate0 + ngate:])
        gates_t = (
            jnp.pad(w_gates.T, ((0, SUBLANE - ngate), (0, 0))).astype(BF16),
            jnp.broadcast_to(jnp.pad(b_gates, (0, SUBLANE - ngate))[:, None], (SUBLANE, LANE)),
        )
        mng = m_norm_g[i].reshape(1, mw)
        convw, convb = conv_w[i], conv_b[i][None, :]
        f1 = (ffn1_wi[i], ffn1_wo[i])
        lng, lnb = ln_g[i], ln_b[i]
        ga0 = gate0 + LANE + 3 * width

        x1, wm, wg, wc, wa, wb, wmix = _ffn_call(
            xp.reshape(nb * seq, d), None, *f1, None, None, lng, lnb, alpha=alpha, ln_row=0, with_embed=False,
            name="prompt_ffn1", rows=FFN1_ROWS, mix_weights=(wi_l, w_a[i], w_b[i], w_mix[i]), gate0=gate0,
            ngate=ngate)
        proj_consts = (wm, wg, wc) + proj_biases
        mix_consts = proj_consts + gates_t + (mng, convw, convb, wa, wb, wmix, lng, lnb)
        (x2, c_p, n_p, m_p, buf_p), (f2_wi, f2_wo, wpg, wpp) = _prompt_mix_call(
            x1.reshape(nb, seq, d), mix_consts, (ffn2_wi[i], ffn2_wo[i], w_pg[i], w_pp[i]), alpha=alpha,
            heads=heads, hdim=hdim, width=width)
        f2 = (f2_wi, f2_wo)
        xp = _ffn_call(x2.reshape(nb * seq, d), p_prompt[i].reshape(nb * seq, -1), *f2, wpg, wpp, lng, lnb,
                       alpha=alpha, ln_row=2, with_embed=True, name="prompt_ffn2", rows=FFN2_ROWS
                       ).reshape(nb, seq, d)

        s1 = _ffn_call(xs, None, *f1, None, None, lng, lnb, alpha=alpha, ln_row=0, with_embed=False,
                       name="sample_ffn1", rows=ns)
        z = _proj_call(s1, *proj_consts)
        ha, hb, c_s, n_s, m_s, buf_s = _sample_step_call(
            z, state_C[i], state_n[i].reshape(ns, mw), state_m[i], state_conv[i].reshape(ns, -1),
            mng, convw, convb, heads=heads, hdim=hdim)
        s2 = _sample_merge_call(s1, ha, hb, z, wa, wb, wmix, lng, lnb, alpha=alpha, ga0=ga0)
        xs = _ffn_call(s2, p_sample[i].reshape(ns, -1), *f2, wpg, wpp, lng, lnb, alpha=alpha, ln_row=2,
                       with_embed=True, name="sample_ffn2", rows=ns)

        for lst, val in zip(outs, (c_p, n_p.reshape(nb, heads, hdim), m_p[:, 0, :heads], buf_p,
                                   c_s, n_s.reshape(ns, heads, hdim), m_s[:, :heads],
                                   buf_s.reshape(ns, CONV_W - 1, width))):
            lst.append(val)
    return (xp, xs.reshape(ns, 1, d)) + tuple(jnp.stack(o) for o in outs)
```

```python
import functools

import jax
import jax.numpy as jnp
from jax import lax
from jax.experimental import pallas as pl
from jax.experimental.pallas import tpu as pltpu

F32 = jnp.float32
BF16 = jnp.bfloat16

LN_EPS = 1e-5
CONV_W = 3
LANE = 128
SUBLANE = 8
BF16_ROWS = 2 * SUBLANE
V7X_VMEM_LIMIT_BYTES = 60 * 1024 * 1024
MLSTM_CHUNK = 256
MIX_ROWS = 512
FFN_ROWS = 512
FFN_COLS = 256


def _dot(a, b):
    return jnp.dot(a.astype(BF16), b.astype(BF16), preferred_element_type=F32)


def _dot_nt(a, b):
    return lax.dot_general(a.astype(BF16), b.astype(BF16), (((1,), (1,)), ((), ())),
                           preferred_element_type=F32)


def _dot_tn(a, b):
    return lax.dot_general(a.astype(BF16), b.astype(BF16), (((0,), (0,)), ((), ())),
                           preferred_element_type=F32)


def _split3(x):
    hi = x.astype(BF16)
    r1 = x - hi.astype(F32)
    mid = r1.astype(BF16)
    lo = (r1 - mid.astype(F32)).astype(BF16)
    return hi, mid, lo


def _layer_norm(x, g, b):
    mu = jnp.mean(x, axis=-1, keepdims=True)
    xc = x - mu
    var = jnp.mean(xc * xc, axis=-1, keepdims=True)
    return xc * lax.rsqrt(var + LN_EPS) * g + b


def _head_norm(h, g):
    mu = jnp.mean(h, axis=-1, keepdims=True)
    hc = h - mu
    var = jnp.mean(hc * hc, axis=-1, keepdims=True)
    return hc * lax.rsqrt(var + LN_EPS) * g


def _sigmoid(x):
    return 1.0 / (1.0 + jnp.exp(-x))


def _log_sigmoid(x):
    return jnp.minimum(x, 0.0) - jnp.log1p(jnp.exp(-jnp.abs(x)))


def _const_spec(shape):
    nd = len(shape)
    return pl.BlockSpec(shape, lambda *_: (0,) * nd, pipeline_mode=pl.Buffered(1))


def _state_step(z_ref, c_ref, n_ref, m_ref, buf_ref, mng_ref, convw_ref, convb_ref,
                ha_ref, hb_ref, co_ref, no_ref, mo_ref, bo_ref, num_ref, qt_ref, kt_ref,
                *, heads, hdim, width):
    bb = z_ref.shape[0]
    mw = heads * hdim
    cb0 = 4 * mw + LANE
    zg = z_ref[:, 4 * mw:4 * mw + LANE]
    lane_id = lax.broadcasted_iota(jnp.int32, (bb, LANE), 1)
    m_out = jnp.zeros((bb, LANE), F32)
    for h in range(heads):
        hs = slice(h * hdim, (h + 1) * hdim)
        q = z_ref[:, hs]
        k = z_ref[:, mw + h * hdim:mw + (h + 1) * hdim] * (hdim ** -0.5)
        v = z_ref[:, 2 * mw + h * hdim:2 * mw + (h + 1) * hdim]
        li = zg[:, h:h + 1]
        lf = _log_sigmoid(zg[:, heads + h:heads + h + 1])
        m_prev = m_ref[:, h:h + 1]
        m_new = jnp.maximum(lf + m_prev, li)
        w_c = jnp.exp(lf + m_prev - m_new)
        kw = k * jnp.exp(li - m_new)
        n_new = w_c * n_ref[:, hs] + kw
        no_ref[:, hs] = n_new
        m_out = jnp.where(lane_id == h, m_new, m_out)
        qt_ref[0:bb, :] = q
        kt_ref[0:bb, :] = kw
        q_t = qt_ref[...].T
        kw_t = kt_ref[...].T
        for j in range(bb):
            c_new = w_c[j:j + 1, :] * c_ref[j, h] + kw_t[:, j:j + 1] * v[j:j + 1, :]
            co_ref[j, h] = c_new
            num_ref[j:j + 1, hs] = jnp.sum(q_t[:, j:j + 1] * c_new, axis=0, keepdims=True)
        den = jnp.sum(q * n_new, axis=1, keepdims=True)
        hh = num_ref[0:bb, hs] * (1.0 / jnp.maximum(jnp.abs(den), jnp.exp(-m_new)))
        o_gate = _sigmoid(z_ref[:, 3 * mw + h * hdim:3 * mw + (h + 1) * hdim])
        ha_ref[:, hs] = _head_norm(hh, mng_ref[:, hs]) * o_gate
    mo_ref[...] = m_out
    pre = z_ref[:, cb0 + width:cb0 + 2 * width] * z_ref[:, cb0 + 2 * width:cb0 + 3 * width]
    conv = convb_ref[...] + pre * convw_ref[CONV_W - 1:CONV_W, :]
    for j in range(CONV_W - 1):
        conv = conv + buf_ref[:, j * width:(j + 1) * width] * convw_ref[j:j + 1, :]
    for j in range(CONV_W - 2):
        bo_ref[:, j * width:(j + 1) * width] = buf_ref[:, (j + 1) * width:(j + 2) * width]
    bo_ref[:, (CONV_W - 2) * width:(CONV_W - 1) * width] = pre
    hb_ref[...] = z_ref[:, cb0:cb0 + width] * conv


def _ffn_kernel(*refs, names, alpha, ln_row, with_embed, n_wc_steps, ngate, state_dims):
    r = dict(zip(names, refs))
    i = pl.program_id(0)
    if n_wc_steps:
        r["wm_t"][...] = r["wt_m"][...].astype(BF16)
        for nm in ("wa", "wb", "wmix"):
            r[nm][...] = r[nm + "32"][...].astype(BF16)

        @pl.when(i < n_wc_steps)
        def _():
            shifted = jnp.concatenate([r["wt_ca"][ngate:, :], r["wt_cb"][0:ngate, :]], axis=0)
            r["wc_t"][...] = shifted.astype(BF16)
    if state_dims is not None:
        heads, hdim, width = state_dims

        @pl.when(i == 0)
        def _():
            r["qt"][...] = jnp.zeros(r["qt"].shape, F32)
            r["kt"][...] = jnp.zeros(r["kt"].shape, F32)

        _state_step(r["z"].at[0], r["c_in"], r["n_in"].at[0], r["m_in"].at[0], r["buf_in"].at[0],
                    r["mng"], r["convw"], r["convb"], r["ha"].at[0], r["hb"].at[0], r["c_out"],
                    r["n_out"].at[0], r["m_out"].at[0], r["buf_out"].at[0], r["num"], r["qt"], r["kt"],
                    heads=heads, hdim=hdim, width=width)

    wi_ref, wo_ref, lng_ref, lnb_ref = r["wi"], r["wo"], r["lng"], r["lnb"]
    d_ff = wo_ref.shape[0]
    x = r["x"][...]
    xb = x.astype(BF16)
    acc = jnp.zeros(x.shape, F32)
    for c0 in range(0, d_ff, FFN_COLS):
        g = jnp.dot(xb, wi_ref[:, c0:c0 + FFN_COLS].astype(BF16), preferred_element_type=F32)
        u = jnp.dot(xb, wi_ref[:, d_ff + c0:d_ff + c0 + FFN_COLS].astype(BF16), preferred_element_type=F32)
        a = (g * _sigmoid(g)) * u
        acc = acc + jnp.dot(a.astype(BF16), wo_ref[c0:c0 + FFN_COLS, :].astype(BF16),
                            preferred_element_type=F32)
    y = _layer_norm(alpha * x + 0.5 * acc, lng_ref[ln_row:ln_row + 1, :], lnb_ref[ln_row:ln_row + 1, :])
    if with_embed:
        gate = _sigmoid(jnp.dot(y.astype(BF16), r["wpg"][...], preferred_element_type=F32))
        emb = jnp.dot(r["p"][...].astype(BF16), r["wpp"][...], preferred_element_type=F32)
        y = _layer_norm(alpha * y + gate * emb, lng_ref[ln_row + 1:ln_row + 2, :],
                        lnb_ref[ln_row + 1:ln_row + 2, :])
    r["y"][...] = y


def _wc_steps(wc_rows, gate0, steps):
    for k in range(steps, 0, -1):
        rows = wc_rows // k
        if wc_rows % k == 0 and rows % BF16_ROWS == 0 and gate0 % rows == 0:
            return k
    raise ValueError("no aligned split of the in-projection rows")


def _ffn_call(x, p, wi, wo, wpg, wpp, lng, lnb, *, alpha, ln_row, name, rows=FFN_ROWS, mix_weights=None,
              gate0=0, ngate=0, state=None, state_dims=None):
    n, d = x.shape
    rows = min(rows, n)
    steps = n // rows
    assert n % rows == 0 and wo.shape[0] % FFN_COLS == 0
    with_embed = p is not None
    row_spec = lambda r_, w: pl.BlockSpec((r_, w), lambda i: (i, 0))

    ins = [("x", x, row_spec(rows, d))]
    if with_embed:
        ins.append(("p", p, row_spec(rows, p.shape[1])))
    ins += [("wi", wi, _const_spec(wi.shape)), ("wo", wo, _const_spec(wo.shape))]
    if with_embed:
        ins += [("wpg", wpg, _const_spec(wpg.shape)), ("wpp", wpp, _const_spec(wpp.shape))]
    ins += [("lng", lng, _const_spec(lng.shape)), ("lnb", lnb, _const_spec(lnb.shape))]
    outs = [("y", jax.ShapeDtypeStruct((n, d), F32), row_spec(rows, d))]
    scratch = []

    n_wc_steps = 0
    if mix_weights is not None:
        w_t = mix_weights[0]
        n_feat, k_in = w_t.shape
        wc_rows = n_feat - gate0 - ngate
        assert ngate == SUBLANE and gate0 % (steps * BF16_ROWS) == 0
        n_wc_steps = _wc_steps(wc_rows, gate0, steps)
        rm, rc = gate0 // steps, wc_rows // n_wc_steps
        last = n_wc_steps - 1
        ins += [
            ("wt_m", w_t, row_spec(rm, k_in)),
            ("wt_ca", w_t, pl.BlockSpec((rc, k_in), lambda i: (gate0 // rc + jnp.minimum(i, last), 0))),
            ("wt_cb", w_t, pl.BlockSpec(
                (SUBLANE, k_in),
                lambda i: (gate0 // SUBLANE + (rc // SUBLANE) * (jnp.minimum(i, last) + 1), 0))),
        ]
        outs += [
            ("wm_t", jax.ShapeDtypeStruct((gate0, k_in), BF16), row_spec(rm, k_in)),
            ("wc_t", jax.ShapeDtypeStruct((wc_rows, k_in), BF16),
             pl.BlockSpec((rc, k_in), lambda i: (jnp.minimum(i, last), 0))),
        ]
        for nm, w in zip(("wa", "wb", "wmix"), mix_weights[1:]):
            wr = w.shape[0] // steps
            assert w.shape[0] % steps == 0 and wr % BF16_ROWS == 0
            ins.append((nm + "32", w, row_spec(wr, w.shape[1])))
            outs.append((nm, jax.ShapeDtypeStruct(w.shape, BF16), row_spec(wr, w.shape[1])))

    if state is not None:
        z, c0, n0, m0, buf0, mng, convw, convb = state
        heads, hdim, width = state_dims
        ns = z.shape[0]
        bb = ns // steps
        assert ns % steps == 0 and bb <= SUBLANE
        blk3 = lambda w: pl.BlockSpec((1, bb, w), lambda i: (i, 0, 0))
        c_spec = pl.BlockSpec((bb, heads, hdim, hdim), lambda i: (i, 0, 0, 0))
        as3 = lambda a: a.reshape(steps, bb, -1)
        mw = heads * hdim
        ins += [("z", as3(z), blk3(z.shape[1])), ("c_in", c0, c_spec), ("n_in", as3(n0), blk3(mw)),
                ("m_in", as3(m0), blk3(heads)), ("buf_in", as3(buf0), blk3((CONV_W - 1) * width)),
                ("mng", mng, _const_spec(mng.shape)), ("convw", convw, _const_spec(convw.shape)),
                ("convb", convb, _const_spec(convb.shape))]
        sds3 = lambda w: jax.ShapeDtypeStruct((steps, bb, w), F32)
        outs += [("ha", sds3(mw), blk3(mw)), ("hb", sds3(width), blk3(width)),
                 ("c_out", jax.ShapeDtypeStruct(c0.shape, F32), c_spec), ("n_out", sds3(mw), blk3(mw)),
                 ("m_out", sds3(LANE), blk3(LANE)),
                 ("buf_out", sds3((CONV_W - 1) * width), blk3((CONV_W - 1) * width))]
        scratch = [("num", pltpu.VMEM((SUBLANE, mw), F32)), ("qt", pltpu.VMEM((LANE, hdim), F32)),
                   ("kt", pltpu.VMEM((LANE, hdim), F32))]

    names = tuple(nm for nm, _, _ in ins) + tuple(nm for nm, _, _ in outs) + tuple(nm for nm, _ in scratch)
    kern = functools.partial(_ffn_kernel, names=names, alpha=alpha, ln_row=ln_row, with_embed=with_embed,
                             n_wc_steps=n_wc_steps, ngate=ngate, state_dims=state_dims if state is not None else None)
    res = pl.pallas_call(
        kern,
        grid=(steps,),
        in_specs=[s for _, _, s in ins],
        out_specs=tuple(s for _, _, s in outs),
        out_shape=tuple(sd for _, sd, _ in outs),
        scratch_shapes=[s for _, s in scratch],
        compiler_params=pltpu.CompilerParams(dimension_semantics=("arbitrary",),
                                             vmem_limit_bytes=V7X_VMEM_LIMIT_BYTES),
        name=name,
    )(*[a for _, a, _ in ins])
    return dict(zip((nm for nm, _, _ in outs), res))


def _merge_tail(x, ha, hb, z_ga, z_gb, wa_ref, wb_ref, wmix_ref, lng_ref, lnb_ref, alpha):
    y_a = jnp.dot(ha.astype(BF16), wa_ref[...], preferred_element_type=F32)
    y_b = jnp.dot(hb.astype(BF16), wb_ref[...], preferred_element_type=F32)
    merged = _sigmoid(z_ga) * y_a + _sigmoid(z_gb) * y_b
    mix = jnp.dot(merged.astype(BF16), wmix_ref[...], preferred_element_type=F32)
    return _layer_norm(alpha * x + mix, lng_ref[1:2, :], lnb_ref[1:2, :])


def _prompt_mix_kernel(x_ref, wm_ref, wg_ref, wc_ref, bm_ref, bg_ref, bc_ref, bgt_ref,
                       mng_ref, convw_ref, convb_ref, wa_ref, wb_ref, wmix_ref, lng_ref, lnb_ref,
                       *rest, alpha, heads, hdim, width, chunk, cast_blocks):
    n_cast = len(cast_blocks)
    cast_in, rest = rest[:n_cast], rest[n_cast:]
    o_ref, c_ref, n_ref, m_ref, conv_ref = rest[:5]
    cast_out, pre_ref = rest[5:5 + n_cast], rest[5 + n_cast]
    t = pl.program_id(1)
    rows, d = x_ref.shape[1], x_ref.shape[2]
    mw = heads * hdim
    carry0 = SUBLANE - (CONV_W - 1)

    @pl.when(t == 0)
    def _():
        c_ref[...] = jnp.zeros(c_ref.shape, F32)
        n_ref[...] = jnp.zeros(n_ref.shape, F32)
        m_ref[...] = jnp.zeros(m_ref.shape, F32)
        pre_ref[0:SUBLANE, :] = jnp.zeros((SUBLANE, width), F32)

    for src, dst, k in zip(cast_in, cast_out, cast_blocks):
        @pl.when(t < k)
        def _(src=src, dst=dst):
            dst[...] = src[...].astype(BF16)

    x = x_ref[0]
    xb = x.astype(BF16)

    def proj(wt_ref, b_ref, c0, w):
        return _dot_nt(xb, wt_ref[c0:c0 + w, :]) + b_ref[:, c0:c0 + w]

    gcol = proj(wg_ref, bg_ref, 0, LANE)
    grow = _dot_nt(wg_ref[0:BF16_ROWS, :], xb) + bgt_ref[:, 0:1]
    lf_col = _log_sigmoid(gcol)
    lf_row = _log_sigmoid(grow)
    r_id = lax.broadcasted_iota(jnp.int32, (chunk, chunk), 0)
    c_id = lax.broadcasted_iota(jnp.int32, (chunk, chunk), 1)
    causal = c_id <= r_id
    lower = causal.astype(BF16)
    upper = (r_id <= c_id).astype(BF16)
    lane_id = lax.broadcasted_iota(jnp.int32, (1, LANE), 1)
    starts = list(range(0, rows, chunk))
    b_cols = [sum(jnp.dot(lower, part, preferred_element_type=F32) for part in _split3(lf_col[r0:r0 + chunk]))
              for r0 in starts]
    b_rows = [sum(jnp.dot(part, upper, preferred_element_type=F32) for part in _split3(lf_row[:, r0:r0 + chunk]))
              for r0 in starts]

    ha_parts = []
    for h in range(heads):
        hs = slice(h * hdim, (h + 1) * hdim)
        q_all = proj(wm_ref, bm_ref, h * hdim, hdim)
        k_all = proj(wm_ref, bm_ref, mw + h * hdim, hdim) * (hdim ** -0.5)
        v_all = proj(wm_ref, bm_ref, 2 * mw + h * hdim, hdim)
        hh_parts = []
        for ci, r0 in enumerate(starts):
            rs = slice(r0, r0 + chunk)
            q, k, v = q_all[rs], k_all[rs], v_all[rs]
            b_col = b_cols[ci][:, heads + h:heads + h + 1]
            b_row = b_rows[ci][heads + h:heads + h + 1, :]
            li_c = gcol[rs, h:h + 1]
            li_r = grow[h:h + 1, rs]
            m_all = m_ref[0]
            m_prev = m_all[:, h:h + 1]
            c_h = c_ref[0, h]
            n_h = n_ref[0, :, hs]

            dmat = jnp.where(causal, b_col - b_row + li_r, -jnp.inf)
            inter = b_col + m_prev
            m_row = jnp.maximum(inter, jnp.max(dmat, axis=1, keepdims=True))
            w_inter = jnp.exp(inter - m_row)
            s = _dot_nt(q, k) * jnp.exp(dmat - m_row)
            num = w_inter * _dot(q, c_h) + _dot(s, v)
            den = w_inter * jnp.sum(q * n_h, axis=1, keepdims=True) + jnp.sum(s, axis=1, keepdims=True)
            hh_parts.append(num * (1.0 / jnp.maximum(jnp.abs(den), jnp.exp(-m_row))))

            b_end = b_col[chunk - 1:chunk, :]
            dec = b_end - b_col + li_c
            m_new = jnp.maximum(b_end + m_prev, jnp.max(dec, axis=0, keepdims=True))
            w_c = jnp.exp(b_end + m_prev - m_new)
            kw = k * jnp.exp(dec - m_new)
            c_ref[0, h] = w_c * c_h + _dot_tn(kw, v)
            n_ref[0, :, hs] = w_c * n_h + jnp.sum(kw, axis=0, keepdims=True)
            m_ref[0] = jnp.where(lane_id == h, m_new, m_all)
        hh = jnp.concatenate(hh_parts, axis=0) if len(hh_parts) > 1 else hh_parts[0]
        o_gate = _sigmoid(proj(wm_ref, bm_ref, 3 * mw + h * hdim, hdim))
        ha_parts.append(_head_norm(hh, mng_ref[:, hs]) * o_gate)
    ha = jnp.concatenate(ha_parts, axis=1)

    pre = proj(wc_ref, bc_ref, width, width) * proj(wc_ref, bc_ref, 2 * width, width)
    pre_ref[SUBLANE:SUBLANE + rows, :] = pre
    conv = convb_ref[...] + pre * convw_ref[CONV_W - 1:CONV_W, :]
    for j in range(CONV_W - 1):
        conv = conv + pre_ref[carry0 + j:carry0 + j + rows, :] * convw_ref[j:j + 1, :]
    tail = pre_ref[rows + carry0:rows + SUBLANE, :]
    pre_ref[carry0:SUBLANE, :] = tail
    conv_ref[0] = tail
    hb = proj(wc_ref, bc_ref, 0, width) * conv

    o_ref[0] = _merge_tail(x, ha, hb, proj(wc_ref, bc_ref, 3 * width, d), proj(wc_ref, bc_ref, 3 * width + d, d),
                           wa_ref, wb_ref, wmix_ref, lng_ref, lnb_ref, alpha)


def _cast_blocks_per_seq(n_rows, nb, steps):
    for k in range(steps, 0, -1):
        if n_rows % (nb * k) == 0 and (n_rows // (nb * k)) % BF16_ROWS == 0:
            return k
    raise ValueError(f"cannot split {n_rows} weight rows over {nb} sequences")


def _prompt_mix_call(x, consts, cast_weights, *, alpha, heads, hdim, width):
    nb, seq, d = x.shape
    chunk = min(MLSTM_CHUNK, seq)
    rows = min(MIX_ROWS, seq)
    steps = seq // rows
    assert seq % rows == 0 and rows % chunk == 0 and chunk % LANE == 0
    cast_blocks = tuple(_cast_blocks_per_seq(w.shape[0], nb, steps) for w in cast_weights)
    kern = functools.partial(_prompt_mix_kernel, alpha=alpha, heads=heads, hdim=hdim, width=width, chunk=chunk,
                             cast_blocks=cast_blocks)
    out_shape = (
        jax.ShapeDtypeStruct((nb, seq, d), F32),
        jax.ShapeDtypeStruct((nb, heads, hdim, hdim), F32),
        jax.ShapeDtypeStruct((nb, 1, heads * hdim), F32),
        jax.ShapeDtypeStruct((nb, 1, LANE), F32),
        jax.ShapeDtypeStruct((nb, CONV_W - 1, width), F32),
    ) + tuple(jax.ShapeDtypeStruct(w.shape, BF16) for w in cast_weights)

    def cast_spec(w, k):
        return pl.BlockSpec((w.shape[0] // (nb * k), w.shape[1]), lambda b, t: (b * k + jnp.minimum(t, k - 1), 0))

    cast_specs = tuple(cast_spec(w, k) for w, k in zip(cast_weights, cast_blocks))
    out_specs = (
        pl.BlockSpec((1, rows, d), lambda b, t: (b, t, 0)),
        pl.BlockSpec((1, heads, hdim, hdim), lambda b, t: (b, 0, 0, 0)),
        pl.BlockSpec((1, 1, heads * hdim), lambda b, t: (b, 0, 0)),
        pl.BlockSpec((1, 1, LANE), lambda b, t: (b, 0, 0)),
        pl.BlockSpec((1, CONV_W - 1, width), lambda b, t: (b, 0, 0)),
    ) + cast_specs
    res = pl.pallas_call(
        kern,
        grid=(nb, steps),
        in_specs=[pl.BlockSpec((1, rows, d), lambda b, t: (b, t, 0))] + [_const_spec(a.shape) for a in consts]
        + list(cast_specs),
        out_specs=out_specs,
        out_shape=out_shape,
        scratch_shapes=[pltpu.VMEM((rows + SUBLANE, width), F32)],
        compiler_params=pltpu.CompilerParams(dimension_semantics=("arbitrary", "arbitrary"),
                                             vmem_limit_bytes=V7X_VMEM_LIMIT_BYTES),
        name="prompt_mix",
    )(x, *consts, *cast_weights)
    return res[:5], res[5:]


def _proj_kernel(x_ref, wm_ref, wg_ref, wc_ref, bm_ref, bg_ref, bc_ref, o_ref):
    xb = x_ref[...].astype(BF16)
    c0 = 0
    for wt_ref, b_ref in ((wm_ref, bm_ref), (wg_ref, bg_ref), (wc_ref, bc_ref)):
        w = wt_ref.shape[0]
        o_ref[:, c0:c0 + w] = _dot_nt(xb, wt_ref[...]) + b_ref[...]
        c0 += w


def _proj_call(x, wm, wg, wc, bm, bg, bc):
    n = x.shape[0]
    return pl.pallas_call(
        _proj_kernel,
        out_shape=jax.ShapeDtypeStruct((n, wm.shape[0] + wg.shape[0] + wc.shape[0]), F32),
        compiler_params=pltpu.CompilerParams(vmem_limit_bytes=V7X_VMEM_LIMIT_BYTES),
        name="sample_proj",
    )(x, wm, wg, wc, bm, bg, bc)


def _sample_merge_kernel(x_ref, ha_ref, hb_ref, z_ref, wa_ref, wb_ref, wmix_ref, lng_ref, lnb_ref, o_ref,
                         *, alpha, ga0):
    d = x_ref.shape[1]
    o_ref[...] = _merge_tail(x_ref[...], ha_ref[...], hb_ref[...], z_ref[:, ga0:ga0 + d],
                             z_ref[:, ga0 + d:ga0 + 2 * d], wa_ref, wb_ref, wmix_ref, lng_ref, lnb_ref, alpha)


def _sample_merge_call(x, ha, hb, z, wa, wb, wmix, lng, lnb, *, alpha, ga0):
    kern = functools.partial(_sample_merge_kernel, alpha=alpha, ga0=ga0)
    return pl.pallas_call(
        kern,
        out_shape=jax.ShapeDtypeStruct(x.shape, F32),
        compiler_params=pltpu.CompilerParams(vmem_limit_bytes=V7X_VMEM_LIMIT_BYTES),
        name="sample_merge",
    )(x, ha, hb, z, wa, wb, wmix, lng, lnb)


def kernel(x_prompt, x_sample, p_prompt, p_sample, state_C, state_n, state_m, state_conv, w_in, b_in, m_norm_g, w_a, w_b, conv_w, conv_b, w_mix, ffn1_wi, ffn1_wo, ffn2_wi, ffn2_wo, w_pg, w_pp, ln_g, ln_b):
    depth = w_in.shape[0]
    nb, seq, d = x_prompt.shape
    ns = x_sample.shape[0]
    assert x_sample.shape[1] == 1
    heads, hdim = m_norm_g.shape[1], m_norm_g.shape[2]
    mw = heads * hdim
    width = conv_w.shape[2]
    ngate = 2 * heads
    alpha = (2 * depth) ** 0.25
    gate0 = 4 * mw

    xp = x_prompt
    xs = x_sample.reshape(ns, d)
    outs = [[] for _ in range(8)]
    for i in range(depth):
        w_t, bi_l = jnp.transpose(w_in[i]), b_in[i]
        b_gates = bi_l[gate0:gate0 + ngate]
        wg_t = jnp.pad(w_t[gate0:gate0 + ngate], ((0, LANE - ngate), (0, 0))).astype(BF16)
        proj_biases = (bi_l[None, :gate0], jnp.pad(b_gates, (0, LANE - ngate))[None, :], bi_l[None, gate0 + ngate:])
        bg_rows = jnp.broadcast_to(jnp.pad(b_gates, (0, BF16_ROWS - ngate))[:, None], (BF16_ROWS, LANE))
        mng = m_norm_g[i].reshape(1, mw)
        convw, convb = conv_w[i], conv_b[i][None, :]
        f1 = (ffn1_wi[i], ffn1_wo[i])
        lng, lnb = ln_g[i], ln_b[i]
        ga0 = gate0 + LANE + 3 * width

        o1 = _ffn_call(xp.reshape(nb * seq, d), None, *f1, None, None, lng, lnb, alpha=alpha, ln_row=0,
                       name="prompt_ffn1", mix_weights=(w_t, w_a[i], w_b[i], w_mix[i]), gate0=gate0, ngate=ngate)
        proj_consts = (o1["wm_t"], wg_t, o1["wc_t"]) + proj_biases
        s1 = _ffn_call(xs, None, *f1, None, None, lng, lnb, alpha=alpha, ln_row=0, name="sample_ffn1")["y"]
        z = _proj_call(s1, *proj_consts)

        mix_consts = proj_consts + (bg_rows, mng, convw, convb, o1["wa"], o1["wb"], o1["wmix"], lng, lnb)
        (x2, c_p, n_p, m_p, buf_p), (f2_wi, f2_wo, wpg, wpp) = _prompt_mix_call(
            o1["y"].reshape(nb, seq, d), mix_consts, (ffn2_wi[i], ffn2_wo[i], w_pg[i], w_pp[i]), alpha=alpha,
            heads=heads, hdim=hdim, width=width)

        o2 = _ffn_call(x2.reshape(nb * seq, d), p_prompt[i].reshape(nb * seq, -1), f2_wi, f2_wo, wpg, wpp, lng, lnb,
                       alpha=alpha, ln_row=2, name="prompt_ffn2",
                       state=(z, state_C[i], state_n[i].reshape(ns, mw), state_m[i], state_conv[i].reshape(ns, -1),
                              mng, convw, convb), state_dims=(heads, hdim, width))
        xp = o2["y"].reshape(nb, seq, d)
        s2 = _sample_merge_call(s1, o2["ha"].reshape(ns, mw), o2["hb"].reshape(ns, width), z, o1["wa"], o1["wb"],
                                o1["wmix"], lng, lnb, alpha=alpha, ga0=ga0)
        xs = _ffn_call(s2, p_sample[i].reshape(ns, -1), f2_wi, f2_wo, wpg, wpp, lng, lnb, alpha=alpha, ln_row=2,
                       name="sample_ffn2")["y"]

        for lst, val in zip(outs, (c_p, n_p.reshape(nb, heads, hdim), m_p[:, 0, :heads], buf_p,
                                   o2["c_out"], o2["n_out"].reshape(ns, heads, hdim),
                                   o2["m_out"].reshape(ns, LANE)[:, :heads],
                                   o2["buf_out"].reshape(ns, CONV_W - 1, width))):
            lst.append(val)
    return (xp, xs.reshape(ns, 1, d)) + tuple(jnp.stack(o) for o in outs)
```
